```python
import jax, jax.numpy as jnp
from jax import lax
import numpy as np

D_MODEL = 1024
BATCH = 8
SEQ = 4096
DEPTH = 2
DEC_BATCH = 128
DEC_SEQ = 1
PAST_LEN = 16384
PAGE_SIZE = 128

HEAD_DIM = 64
ROPE_THETA = 10000.0
Q_BLOCK = 128
SEL_Q_BLOCK = 32
A_HEADS = 8
A_KV_HEADS = 2
IDX_HEADS = 8
IDX_DIM = 64
DSA_TOPK = 256
B_HEADS = 8
B_KV_HEADS = 2
CMP_BLOCK = 64
SEL_BLOCK = 64
SEL_TOPN = 16
WINDOW = 512
C_HEADS = 16
C_NOPE = 64
C_ROPE = 32
C_VDIM = 64
C_Q_RANK = 384
C_KV_RANK = 256
FFN_DIM = 2752
LN_EPS = 1e-5
RMS_EPS = 1e-6
DN_ALPHA = (2 * DEPTH) ** 0.25
DN_BETA = (8 * DEPTH) ** -0.25
AB_WIDTHS = (A_HEADS * HEAD_DIM, A_KV_HEADS * HEAD_DIM, A_KV_HEADS * HEAD_DIM,
             IDX_HEADS * IDX_DIM, IDX_DIM, IDX_HEADS, B_HEADS * HEAD_DIM) + (B_KV_HEADS * HEAD_DIM,) * 6 + (3 * B_HEADS,)
AB_SPLITS = tuple(int(v) for v in np.cumsum(AB_WIDTHS)[:-1])
AB_IN_WIDTH = sum(AB_WIDTHS)
AB_MIX_WIDTH = (A_HEADS + B_HEADS) * HEAD_DIM
MLA_DOWN_WIDTH = C_Q_RANK + C_KV_RANK + C_ROPE
F32 = jnp.float32

kernel_name = 'hybrid_dsa_nsa_mla_decoder_step'


def layer_norm(x, g, b):
    xf = x.astype(F32)
    mu = jnp.mean(xf, -1, keepdims=True)
    var = jnp.mean(jnp.square(xf - mu), -1, keepdims=True)
    return ((xf - mu) * lax.rsqrt(var + LN_EPS) * g + b).astype(x.dtype)


def rms_norm(x, g):
    xf = x.astype(F32)
    return (xf * lax.rsqrt(jnp.mean(xf * xf, -1, keepdims=True) + RMS_EPS) * g).astype(x.dtype)


def post_norm(x, f, g, b):
    return layer_norm(DN_ALPHA * x + f, g, b)


def half_ffn(x, fw, j):
    w_gate, w_up, w_down = fw[0][j], fw[1][j], fw[2][j]
    return 0.5 * ((jax.nn.silu(x @ w_gate) * (x @ w_up)) @ w_down)


def rope(x, pos):
    d = x.shape[-1]
    inv = jnp.power(ROPE_THETA, -jnp.arange(0, d, 2, dtype=F32) / d)
    ang = pos.astype(F32)[:, None] * inv[None, :]
    shape = (pos.shape[0],) + (1,) * (x.ndim - 3) + (d // 2,)
    cos, sin = jnp.cos(ang).reshape(shape), jnp.sin(ang).reshape(shape)
    x1, x2 = jnp.split(x.astype(F32), 2, axis=-1)
    return jnp.concatenate([x1 * cos - x2 * sin, x2 * cos + x1 * sin], -1).astype(x.dtype)


def masked_softmax(logits, mask):
    logits = jnp.where(mask, logits, -jnp.inf)
    m = jnp.max(logits, -1, keepdims=True)
    e = jnp.exp(logits - jnp.where(jnp.isfinite(m), m, 0.0))
    s = jnp.sum(e, -1, keepdims=True)
    return e / jnp.where(s > 0, s, 1.0)


def unblock(o):
    nb, b, q = o.shape[:3]
    return jnp.moveaxis(o, 0, 1).reshape(b, nb * q, -1)


def take_rows(rows, pos, head=None):
    bidx = jnp.arange(rows.shape[0]).reshape((-1,) + (1,) * (pos.ndim - 1))
    return rows[bidx, pos] if head is None else rows[bidx, pos, :, head]


def gather_past(pool, page_table):
    b, n_pages = page_table.shape
    return pool[page_table].reshape((b, n_pages * PAGE_SIZE) + pool.shape[2:])


def fetch_paged(pool, page_table, new_rows, pos, head=None):
    past = page_table.shape[1] * PAGE_SIZE
    bidx = jnp.arange(pos.shape[0]).reshape((-1,) + (1,) * (pos.ndim - 1))
    pp = jnp.clip(pos, 0, past - 1)
    page = page_table[bidx, pp // PAGE_SIZE]
    off = pp % PAGE_SIZE
    pn = jnp.clip(pos - past, 0, new_rows.shape[1] - 1)
    if head is None:
        old, new = pool[page, off], new_rows[bidx, pn]
    else:
        old, new = pool[page, off, :, head], new_rows[bidx, pn, :, head]
    in_past = (pos < past).reshape(pos.shape + (1,) * (old.ndim - pos.ndim))
    return jnp.where(in_past, old, new)


def block_means(rows):
    b, l = rows.shape[:2]
    nc = l // CMP_BLOCK
    m = rows[:, :nc * CMP_BLOCK].reshape((b, nc, CMP_BLOCK) + rows.shape[2:]).astype(F32).mean(2)
    return m.astype(rows.dtype)


def gathered_attend(q, k, v, mask):
    b, t, h, d = q.shape
    kvh = k.shape[2]
    qg = q.reshape(b, t, kvh, h // kvh, d)
    logits = jnp.einsum('bthgd,bthnd->bthgn', qg, k).astype(F32) * d ** -0.5
    p = masked_softmax(logits, mask[:, :, :, None, :])
    o = jnp.einsum('bthgn,bthnd->bthgd', p.astype(v.dtype), v)
    return o.reshape(b, t, h, d)


def ab_project(x, pos, w_in):
    b, t, _ = x.shape
    (aq, ak, av, iq, ik, iw, bq, ck, cv, sk, sv, wk, wv, bg) = jnp.split(x @ w_in, AB_SPLITS, axis=-1)
    heads = lambda a, n, d: a.reshape(b, t, n, d)

    def kv_pair(k_, v_, n):
        return jnp.stack([rope(heads(k_, n, HEAD_DIM), pos), heads(v_, n, HEAD_DIM)], axis=2)

    return dict(
        aq=rope(heads(aq, A_HEADS, HEAD_DIM), pos),
        a_kv=kv_pair(ak, av, A_KV_HEADS),
        iq=rope(heads(iq, IDX_HEADS, IDX_DIM), pos),
        ik=rope(ik, pos),
        iw=iw * IDX_HEADS ** -0.5,
        bq=rope(heads(bq, B_HEADS, HEAD_DIM), pos),
        cmp_kv=kv_pair(ck, cv, B_KV_HEADS),
        sel_kv=kv_pair(sk, sv, B_KV_HEADS),
        win_kv=kv_pair(wk, wv, B_KV_HEADS),
        gates=jax.nn.sigmoid(bg).reshape(b, t, 3, B_HEADS),
    )


def indexer_topk(iq, iw, qpos, segs, k_top):
    scores = []
    for ik, kpos in segs:
        dots = jnp.einsum('bthd,bsd->bths', iq, ik).astype(F32) * IDX_DIM ** -0.5
        s = jnp.einsum('bths,bth->bts', jax.nn.relu(dots), iw.astype(F32))
        scores.append(jnp.where(kpos[None, None, :] <= qpos[None, :, None], s, -jnp.inf))
    return lax.top_k(jnp.concatenate(scores, -1), k_top)[1]


def dsa_attend(q, rows, valid):
    k = jnp.swapaxes(rows[..., 0, :, :], 2, 3)
    v = jnp.swapaxes(rows[..., 1, :, :], 2, 3)
    return gathered_attend(q, k, v, valid[:, :, None, :])


def nsa_attend(q, gates, qpos, kc, fetch_sel, n_keys, win_kv, win_pos):
    b, t, h, d = q.shape
    g = h // B_KV_HEADS
    qg = q.reshape(b, t, B_KV_HEADS, g, d)
    scale = d ** -0.5
    nc = kc.shape[1]
    c_end = (jnp.arange(nc) + 1) * CMP_BLOCK - 1
    lc = jnp.einsum('bthgd,bchd->bthgc', qg, kc[:, :, 0]).astype(F32) * scale
    pc = masked_softmax(lc, (c_end[None, :] <= qpos[:, None])[None, :, None, None, :])
    o_c = jnp.einsum('bthgc,bchd->bthgd', pc.astype(q.dtype), kc[:, :, 1]).reshape(b, t, h, d)
    n_blk = -(-n_keys // SEL_BLOCK)
    blk = jnp.arange(n_blk)
    cur = (qpos // SEL_BLOCK)[:, None, None]
    imp = jnp.pad(pc.sum(3), ((0, 0), (0, 0), (0, 0), (0, n_blk - nc)))
    imp = jnp.where((blk == 0) | (blk == cur), jnp.inf, imp)
    imp = jnp.where(blk > cur, -jnp.inf, imp)
    n_top = min(SEL_TOPN, n_blk)
    sel = lax.top_k(imp, n_top)[1]
    spos = (sel[..., None] * SEL_BLOCK + jnp.arange(SEL_BLOCK)).reshape(b, t, B_KV_HEADS, n_top * SEL_BLOCK)
    rows = fetch_sel(spos, jnp.arange(B_KV_HEADS)[None, None, :, None])
    o_s = gathered_attend(q, rows[..., 0, :], rows[..., 1, :], spos <= qpos[None, :, None, None])
    lw = jnp.einsum('bthgd,bshd->bthgs', qg, win_kv[:, :, 0]).astype(F32) * scale
    dist = qpos[:, None] - win_pos[None, :]
    mw = (dist >= 0) & (dist <= WINDOW) & (win_pos >= 0)[None, :]
    pw = masked_softmax(lw, mw[None, :, None, None, :])
    o_w = jnp.einsum('bthgs,bshd->bthgd', pw.astype(q.dtype), win_kv[:, :, 1]).reshape(b, t, h, d)
    gt = gates[..., None]
    return gt[:, :, 0] * o_c + gt[:, :, 1] * o_s + gt[:, :, 2] * o_w


def ab_mixer_prompt(x, w_in, w_out):
    b, t, _ = x.shape
    pos = jnp.arange(t)
    p = ab_project(x, pos, w_in)
    k_dsa = min(DSA_TOPK, t // 4)
    kc = block_means(p['cmp_kv'])
    win_pad = jnp.pad(p['win_kv'], ((0, 0), (WINDOW, 0), (0, 0), (0, 0), (0, 0)))
    a_kv, sel_kv = p['a_kv'], p['sel_kv']

    def dsa_block(i):
        s0 = i * Q_BLOCK
        qpos = s0 + jnp.arange(Q_BLOCK)
        cut = lambda a: lax.dynamic_slice_in_dim(a, s0, Q_BLOCK, axis=1)
        idx = indexer_topk(cut(p['iq']), cut(p['iw']), qpos, [(p['ik'], pos)], k_dsa)
        return dsa_attend(cut(p['aq']), take_rows(a_kv, idx), idx <= qpos[None, :, None])

    def nsa_block(i):
        s0 = i * SEL_Q_BLOCK
        qpos = s0 + jnp.arange(SEL_Q_BLOCK)
        cut = lambda a: lax.dynamic_slice_in_dim(a, s0, SEL_Q_BLOCK, axis=1)
        win = lax.dynamic_slice_in_dim(win_pad, s0, WINDOW + SEL_Q_BLOCK, axis=1)
        wpos = s0 - WINDOW + jnp.arange(WINDOW + SEL_Q_BLOCK)
        return nsa_attend(cut(p['bq']), cut(p['gates']), qpos, kc,
                          lambda sp, hd: take_rows(sel_kv, sp, hd), t, win, wpos)

    o_a = unblock(lax.map(dsa_block, jnp.arange(t // Q_BLOCK)))
    o_b = unblock(lax.map(nsa_block, jnp.arange(t // SEL_Q_BLOCK)))
    y = jnp.concatenate([o_a, o_b], -1) @ w_out
    return y, (a_kv, p['ik'], p['cmp_kv'], sel_kv, p['win_kv'][:, -min(WINDOW, t):])


def ab_mixer_sample(x, cache_dsa_kv, cache_dsa_idx, cache_nsa_cmp_kv, cache_nsa_sel_kv, state_nsa_win_kv,
                    page_table, w_in, w_out):
    b, t, _ = x.shape
    past = page_table.shape[1] * PAGE_SIZE
    qpos = past + jnp.arange(t)
    n_keys = past + t
    p = ab_project(x, qpos, w_in)
    segs = [(gather_past(cache_dsa_idx, page_table), jnp.arange(past)), (p['ik'], qpos)]
    idx = indexer_topk(p['iq'], p['iw'], qpos, segs, min(DSA_TOPK, n_keys // 4))
    o_a = dsa_attend(p['aq'], fetch_paged(cache_dsa_kv, page_table, p['a_kv'], idx), idx <= qpos[None, :, None])
    kc = jnp.concatenate([block_means(gather_past(cache_nsa_cmp_kv, page_table)), block_means(p['cmp_kv'])], axis=1)
    nbuf = state_nsa_win_kv.shape[1]
    win = jnp.concatenate([state_nsa_win_kv, p['win_kv']], axis=1)
    wpos = jnp.concatenate([past - nbuf + jnp.arange(nbuf), qpos])
    o_b = nsa_attend(p['bq'], p['gates'], qpos, kc,
                     lambda sp, hd: fetch_paged(cache_nsa_sel_kv, page_table, p['sel_kv'], sp, hd),
                     n_keys, win, wpos)
    y = jnp.concatenate([o_a.reshape(b, t, -1), o_b.reshape(b, t, -1)], -1) @ w_out
    return y, (p['a_kv'], p['ik'], p['cmp_kv'], p['sel_kv'], win[:, -nbuf:])


def mla_project(x, pos, w_down, g_q, g_kv, w_uq):
    b, t, _ = x.shape
    cq, ckv, kpe = jnp.split(x @ w_down, [C_Q_RANK, C_Q_RANK + C_KV_RANK], axis=-1)
    q = (rms_norm(cq, g_q) @ w_uq).reshape(b, t, C_HEADS, C_NOPE + C_ROPE)
    q_nope, q_pe = jnp.split(q, [C_NOPE], axis=-1)
    return q_nope, rope(q_pe, pos), rms_norm(ckv, g_kv), rope(kpe, pos)


def mla_prompt(x, w_down, g_q, g_kv, w_uq, w_uk, w_uv, w_out):
    b, t, _ = x.shape
    pos = jnp.arange(t)
    q_nope, q_pe, ckv, kpe = mla_project(x, pos, w_down, g_q, g_kv, w_uq)
    k_nope = jnp.einsum('bsc,chd->bshd', ckv, w_uk)
    v = jnp.einsum('bsc,chd->bshd', ckv, w_uv)
    scale = (C_NOPE + C_ROPE) ** -0.5

    def block(i):
        s0 = i * Q_BLOCK
        qpos = s0 + jnp.arange(Q_BLOCK)
        cut = lambda a: lax.dynamic_slice_in_dim(a, s0, Q_BLOCK, axis=1)
        logits = (jnp.einsum('bthd,bshd->bths', cut(q_nope), k_nope)
                  + jnp.einsum('bthr,bsr->bths', cut(q_pe), kpe)).astype(F32) * scale
        pr = masked_softmax(logits, (pos[None, :] <= qpos[:, None])[None, :, None, :])
        return jnp.einsum('bths,bshd->bthd', pr.astype(x.dtype), v)

    y = unblock(lax.map(block, jnp.arange(t // Q_BLOCK))) @ w_out
    return y, (ckv, kpe)


def mla_sample(x, cache_mla_ckv, cache_mla_kpe, page_table, w_down, g_q, g_kv, w_uq, w_uk, w_uv, w_out):
    b, t, _ = x.shape
    past = page_table.shape[1] * PAGE_SIZE
    qpos = past + jnp.arange(t)
    q_nope, q_pe, ckv, kpe = mla_project(x, qpos, w_down, g_q, g_kv, w_uq)
    q_lat = jnp.einsum('bthd,chd->bthc', q_nope, w_uk)
    ckv_past = gather_past(cache_mla_ckv, page_table)
    kpe_past = gather_past(cache_mla_kpe, page_table)
    scale = (C_NOPE + C_ROPE) ** -0.5
    seg = lambda c, r: (jnp.einsum('bthc,bsc->bths', q_lat, c)
                        + jnp.einsum('bthr,bsr->bths', q_pe, r)).astype(F32) * scale
    mask = jnp.concatenate([jnp.ones((t, past), bool), qpos[None, :] <= qpos[:, None]], -1)
    pr = masked_softmax(jnp.concatenate([seg(ckv_past, kpe_past), seg(ckv, kpe)], -1), mask[None, :, None, :])
    pr = pr.astype(x.dtype)
    o_lat = (jnp.einsum('bths,bsc->bthc', pr[..., :past], ckv_past)
             + jnp.einsum('bths,bsc->bthc', pr[..., past:], ckv))
    o = jnp.einsum('bthc,chd->bthd', o_lat, w_uv)
    y = o.reshape(b, t, -1) @ w_out
    return y, (ckv, kpe)


def setup_inputs(seed: int = 0) -> dict:
    key = jax.random.key(seed)
    ks = jax.random.split(key, 32)
    n_pages = PAST_LEN // PAGE_SIZE
    n_used = DEC_BATCH * n_pages
    n_pool = n_used + (n_used + 3) // 4
    win_buf = min(WINDOW, PAST_LEN)

    def nrm(i, shape, scale=1.0):
        return jax.random.normal(ks[i], shape, F32) * scale

    page_table = jax.random.permutation(ks[0], n_pool)[:n_used].reshape(DEC_BATCH, n_pages).astype(jnp.int32)
    kv_a = (n_pool, PAGE_SIZE, 2, A_KV_HEADS, HEAD_DIM)
    kv_b = (n_pool, PAGE_SIZE, 2, B_KV_HEADS, HEAD_DIM)
    return {
        'x_prompt': nrm(1, (BATCH, SEQ, D_MODEL)),
        'x_sample': nrm(2, (DEC_BATCH, DEC_SEQ, D_MODEL)),
        'cache_dsa_kv': nrm(3, kv_a),
        'cache_dsa_idx': nrm(4, (n_pool, PAGE_SIZE, IDX_DIM)),
        'cache_nsa_cmp_kv': nrm(5, kv_b),
        'cache_nsa_sel_kv': nrm(6, kv_b),
        'state_nsa_win_kv': nrm(7, (DEC_BATCH, win_buf, 2, B_KV_HEADS, HEAD_DIM)),
        'cache_mla_ckv': nrm(8, (n_pool, PAGE_SIZE, C_KV_RANK)),
        'cache_mla_kpe': nrm(9, (n_pool, PAGE_SIZE, C_ROPE)),
        'page_table': page_table,
        'ln_g': 1.0 + nrm(10, (DEPTH, 3, D_MODEL), 0.02),
        'ln_b': nrm(11, (DEPTH, 3, D_MODEL), 0.02),
        'ffn_w_gate': nrm(12, (DEPTH, 2, D_MODEL, FFN_DIM), D_MODEL ** -0.5),
        'ffn_w_up': nrm(13, (DEPTH, 2, D_MODEL, FFN_DIM), D_MODEL ** -0.5),
        'ffn_w_down': nrm(14, (DEPTH, 2, FFN_DIM, D_MODEL), FFN_DIM ** -0.5 * DN_BETA),
        'ab_w_in': nrm(15, (D_MODEL, AB_IN_WIDTH), D_MODEL ** -0.5),
        'ab_w_out': nrm(16, (AB_MIX_WIDTH, D_MODEL), AB_MIX_WIDTH ** -0.5 * DN_BETA),
        'mla_w_down': nrm(17, (D_MODEL, MLA_DOWN_WIDTH), D_MODEL ** -0.5),
        'mla_g_q': 1.0 + nrm(18, (C_Q_RANK,), 0.02),
        'mla_g_kv': 1.0 + nrm(19, (C_KV_RANK,), 0.02),
        'mla_w_uq': nrm(20, (C_Q_RANK, C_HEADS * (C_NOPE + C_ROPE)), C_Q_RANK ** -0.5),
        'mla_w_uk': nrm(21, (C_KV_RANK, C_HEADS, C_NOPE), C_KV_RANK ** -0.5),
        'mla_w_uv': nrm(22, (C_KV_RANK, C_HEADS, C_VDIM), C_KV_RANK ** -0.5),
        'mla_w_out': nrm(23, (C_HEADS * C_VDIM, D_MODEL), (C_HEADS * C_VDIM) ** -0.5 * DN_BETA),
    }


def reference(x_prompt, x_sample, cache_dsa_kv, cache_dsa_idx, cache_nsa_cmp_kv, cache_nsa_sel_kv,
              state_nsa_win_kv, cache_mla_ckv, cache_mla_kpe, page_table,
              ln_g, ln_b, ffn_w_gate, ffn_w_up, ffn_w_down, ab_w_in, ab_w_out,
              mla_w_down, mla_g_q, mla_g_kv, mla_w_uq, mla_w_uk, mla_w_uv, mla_w_out):
    yp, ys = x_prompt, x_sample
    for layer in range(DEPTH):
        fw = (ffn_w_gate[layer], ffn_w_up[layer], ffn_w_down[layer])
        g, b = ln_g[layer], ln_b[layer]
        yp = post_norm(yp, half_ffn(yp, fw, 0), g[0], b[0])
        ys = post_norm(ys, half_ffn(ys, fw, 0), g[0], b[0])
        if layer % 2 == 0:
            mp, (dsa_kv_p, dsa_idx_p, cmp_kv_p, sel_kv_p, win_kv_p) = ab_mixer_prompt(yp, ab_w_in, ab_w_out)
            ms, (dsa_kv_s, dsa_idx_s, cmp_kv_s, sel_kv_s, win_kv_s) = ab_mixer_sample(
                ys, cache_dsa_kv, cache_dsa_idx, cache_nsa_cmp_kv, cache_nsa_sel_kv, state_nsa_win_kv,
                page_table, ab_w_in, ab_w_out)
        else:
            mp, (ckv_p, kpe_p) = mla_prompt(yp, mla_w_down, mla_g_q, mla_g_kv, mla_w_uq, mla_w_uk, mla_w_uv, mla_w_out)
            ms, (ckv_s, kpe_s) = mla_sample(ys, cache_mla_ckv, cache_mla_kpe, page_table, mla_w_down, mla_g_q,
                                            mla_g_kv, mla_w_uq, mla_w_uk, mla_w_uv, mla_w_out)
        yp = post_norm(yp, mp, g[1], b[1])
        ys = post_norm(ys, ms, g[1], b[1])
        yp = post_norm(yp, half_ffn(yp, fw, 1), g[2], b[2])
        ys = post_norm(ys, half_ffn(ys, fw, 1), g[2], b[2])
    return (yp, ys, dsa_kv_p, dsa_kv_s, dsa_idx_p, dsa_idx_s, cmp_kv_p, cmp_kv_s, sel_kv_p, sel_kv_s,
            win_kv_p, win_kv_s, ckv_p, ckv_s, kpe_p, kpe_s)
```

```python
import functools

import jax
import jax.numpy as jnp
import numpy as np
from jax import lax
from jax.experimental import pallas as pl
from jax.experimental.pallas import tpu as pltpu

F32 = jnp.float32
BF16 = jnp.bfloat16
I32 = jnp.int32

HEAD_DIM = 64
ROPE_THETA = 10000.0
PAGE_SIZE = 128
A_HEADS, A_KV_HEADS = 8, 2
IDX_HEADS, IDX_DIM = 8, 64
DSA_TOPK = 256
B_HEADS, B_KV_HEADS = 8, 2
CMP_BLOCK = 64
SEL_BLOCK = 64
SEL_TOPN = 16
WINDOW = 512
C_HEADS, C_NOPE, C_ROPE, C_VDIM = 16, 64, 32, 64
C_Q_RANK, C_KV_RANK = 384, 256
LN_EPS = 1e-5
RMS_EPS = 1e-6
DEPTH = 2
DN_ALPHA = (2 * DEPTH) ** 0.25

LANES = 128
VMEM_LIMIT = 56 * 1024 * 1024
NEG = -1e30
INT_MIN = -2 ** 31

_NT = (((1,), (1,)), ((), ()))
_TN = (((0,), (0,)), ((), ()))


def _cparams(sem):
    return pltpu.CompilerParams(dimension_semantics=sem, vmem_limit_bytes=VMEM_LIMIT)


def _resident(shape):
    nd = len(shape)
    return pl.BlockSpec(shape, lambda *_: (0,) * nd, pipeline_mode=pl.Buffered(1))


def _layer_norm(y, g, b):
    mu = jnp.mean(y, axis=-1, keepdims=True)
    yc = y - mu
    var = jnp.mean(yc * yc, axis=-1, keepdims=True)
    return yc * lax.rsqrt(var + LN_EPS) * g + b


def _swap_rope(v, tab_c, tab_s, first, d_half):
    partner = jnp.where(first, pltpu.roll(v, LANES - d_half, 1), pltpu.roll(v, d_half, 1))
    return v * tab_c + partner * tab_s


def _ffn_ln_kernel(x_ref, wg_ref, wu_ref, wd_ref, g_ref, b_ref, o_ref, *, n_chunks, chunk):
    x = x_ref[...]
    xb = x.astype(BF16)
    acc = None
    for c in range(n_chunks):
        sl = slice(c * chunk, (c + 1) * chunk)
        gate = jnp.dot(xb, wg_ref[:, sl], preferred_element_type=F32)
        up = jnp.dot(xb, wu_ref[:, sl], preferred_element_type=F32)
        h = (gate * jax.nn.sigmoid(gate) * up).astype(BF16)
        part = jnp.dot(h, wd_ref[sl, :], preferred_element_type=F32)
        acc = part if acc is None else acc + part
    y = DN_ALPHA * x + 0.5 * acc
    o_ref[...] = _layer_norm(y, g_ref[...], b_ref[...])


def _prep_ffn(w_gate, w_up, w_down):
    f = w_gate.shape[1]
    fp = -(-f // (2 * LANES)) * (2 * LANES)
    pad = fp - f
    wg = jnp.pad(w_gate, ((0, 0), (0, pad))).astype(BF16)
    wu = jnp.pad(w_up, ((0, 0), (0, pad))).astype(BF16)
    wd = jnp.pad(w_down, ((0, pad), (0, 0))).astype(BF16)
    return wg, wu, wd


def ffn_ln(x, wg, wu, wd, g, b, tm):
    n, d = x.shape
    fp = wg.shape[1]
    n_chunks = 2
    kern = functools.partial(_ffn_ln_kernel, n_chunks=n_chunks, chunk=fp // n_chunks)
    return pl.pallas_call(
        kern,
        grid=(n // tm,),
        in_specs=[
            pl.BlockSpec((tm, d), lambda i: (i, 0)),
            _resident(wg.shape), _resident(wu.shape), _resident(wd.shape),
            _resident((1, d)), _resident((1, d)),
        ],
        out_specs=pl.BlockSpec((tm, d), lambda i: (i, 0)),
        out_shape=jax.ShapeDtypeStruct((n, d), F32),
        compiler_params=_cparams(("parallel",)),
        name="ffn_ln",
    )(x, wg, wu, wd, g.reshape(1, d), b.reshape(1, d))


def _proj_ln_kernel(*refs, n_in):
    x_ref = refs[0]
    o_refs = refs[1:1 + n_in]
    w_refs = refs[1 + n_in:1 + 2 * n_in]
    g_ref, b_ref, y_ref = refs[1 + 2 * n_in:]
    acc = None
    for o_ref, w_ref in zip(o_refs, w_refs):
        part = jnp.dot(o_ref[...], w_ref[...], preferred_element_type=F32)
        acc = part if acc is None else acc + part
    y_ref[...] = _layer_norm(DN_ALPHA * x_ref[...] + acc, g_ref[...], b_ref[...])


def proj_ln(x, outs, ws, g, b, tm):
    n, d = x.shape
    n_in = len(outs)
    kern = functools.partial(_proj_ln_kernel, n_in=n_in)
    in_specs = [pl.BlockSpec((tm, d), lambda i: (i, 0))]
    in_specs += [pl.BlockSpec((tm, o.shape[1]), lambda i: (i, 0)) for o in outs]
    in_specs += [_resident(w.shape) for w in ws]
    in_specs += [_resident((1, d)), _resident((1, d))]
    return pl.pallas_call(
        kern,
        grid=(n // tm,),
        in_specs=in_specs,
        out_specs=pl.BlockSpec((tm, d), lambda i: (i, 0)),
        out_shape=jax.ShapeDtypeStruct((n, d), F32),
        compiler_params=_cparams(("parallel",)),
        name="proj_ln",
    )(x, *outs, *ws, g.reshape(1, d), b.reshape(1, d))


AB_Q_W = 3 * 512
AB_COLS = AB_Q_W + 4 * 256 + 128 + 128
MISC_IW = 0
MISC_GATE = IDX_HEADS


def _ab_column_perm():
    widths = (512, 128, 128, 512, 64, 8, 512, 128, 128, 128, 128, 128, 128, 24)
    offs = np.concatenate([[0], np.cumsum(widths)])
    seg = lambda i: np.arange(offs[i], offs[i + 1])
    (aq, ak, av, iq, ik, iw, bq, ck, cv, sk, sv, wk, wv, bg) = [seg(i) for i in range(14)]
    pad = lambda n: -np.ones(n, np.int64)
    return np.concatenate([aq, iq, bq, ak, av, ck, cv, sk, sv, wk, wv, ik, pad(64), iw, bg, pad(96)])


def _prep_ab_w_in(w_in):
    perm = _ab_column_perm()
    w = jnp.where((perm >= 0)[None, :], jnp.take(w_in, np.maximum(perm, 0), axis=1), 0.0)
    return w.astype(BF16)


def _rope_tables(pos, d, lane_lo, lane_hi):
    inv = jnp.power(ROPE_THETA, -jnp.arange(0, d, 2, dtype=F32) / d)
    ang = pos.astype(F32)[:, None] * inv[None, :]
    cos, sin = jnp.cos(ang), jnp.sin(ang)
    lane = np.arange(LANES)
    inside = (lane >= lane_lo) & (lane < lane_hi)
    j = (lane - lane_lo) % d
    idx = j % (d // 2)
    sign = np.where(j < d // 2, -1.0, 1.0).astype(np.float32)
    tab_c = jnp.where(inside[None, :], cos[:, idx], 1.0)
    tab_s = jnp.where(inside[None, :], sin[:, idx] * sign[None, :], 0.0)
    return tab_c, tab_s


def _ab_proj_kernel(x_ref, w_ref, c_ref, s_ref, q_ref, akv_ref, ckv_ref, skv_ref, wkv_ref, ik_ref, misc_ref):
    xb = x_ref[...].astype(BF16)
    tab_c, tab_s = c_ref[...], s_ref[...]
    lane = lax.broadcasted_iota(I32, tab_c.shape, 1)
    first = (lane % HEAD_DIM) < HEAD_DIM // 2
    rope = lambda v: _swap_rope(v, tab_c, tab_s, first, HEAD_DIM // 2)

    for blk in range(3):
        v = jnp.dot(xb, w_ref[:, blk * 512:(blk + 1) * 512], preferred_element_type=F32)
        for j in range(4):
            q_ref[:, blk * 512 + j * LANES: blk * 512 + (j + 1) * LANES] = (
                rope(v[:, j * LANES:(j + 1) * LANES]) * HEAD_DIM ** -0.5).astype(BF16)
    for n, kv_ref in enumerate((akv_ref, ckv_ref, skv_ref, wkv_ref)):
        c0 = AB_Q_W + n * 256
        v = jnp.dot(xb, w_ref[:, c0:c0 + 256], preferred_element_type=F32)
        kv_ref[:, 0:LANES] = rope(v[:, 0:LANES])
        kv_ref[:, LANES:2 * LANES] = v[:, LANES:2 * LANES]
    c0 = AB_Q_W + 4 * 256
    v = jnp.dot(xb, w_ref[:, c0:c0 + 256], preferred_element_type=F32)
    ik_ref[...] = rope(v[:, 0:LANES])[:, 0:IDX_DIM]
    m = v[:, LANES:2 * LANES]
    mlane = lax.broadcasted_iota(I32, m.shape, 1)
    misc_ref[...] = jnp.where(mlane < IDX_HEADS, m * IDX_HEADS ** -0.5, jax.nn.sigmoid(m))


def ab_proj(x, w_in_b, tab_c, tab_s, tm):
    n, d = x.shape
    n_tab = tab_c.shape[0] // tm
    row = lambda i: (i, 0)
    tab = lambda i: (i % n_tab, 0)
    out_shape = (
        jax.ShapeDtypeStruct((n, AB_Q_W), BF16),
        jax.ShapeDtypeStruct((n, 256), F32), jax.ShapeDtypeStruct((n, 256), F32),
        jax.ShapeDtypeStruct((n, 256), F32), jax.ShapeDtypeStruct((n, 256), F32),
        jax.ShapeDtypeStruct((n, IDX_DIM), F32),
        jax.ShapeDtypeStruct((n, LANES), F32),
    )
    out_specs = (
        pl.BlockSpec((tm, AB_Q_W), row),
        pl.BlockSpec((tm, 256), row), pl.BlockSpec((tm, 256), row),
        pl.BlockSpec((tm, 256), row), pl.BlockSpec((tm, 256), row),
        pl.BlockSpec((tm, IDX_DIM), row),
        pl.BlockSpec((tm, LANES), row),
    )
    return pl.pallas_call(
        _ab_proj_kernel,
        grid=(n // tm,),
        in_specs=[pl.BlockSpec((tm, d), row), _resident(w_in_b.shape),
                  pl.BlockSpec((tm, LANES), tab), pl.BlockSpec((tm, LANES), tab)],
        out_specs=out_specs,
        out_shape=out_shape,
        compiler_params=_cparams(("parallel",)),
        name="ab_proj",
    )(x, w_in_b, tab_c, tab_s)


def _sortable_key(s):
    bits = pltpu.bitcast(s, I32)
    return bits ^ ((bits >> 31) & 0x7FFFFFFF)


def _online_softmax_step(carry, logits, v):
    m, l, acc = carry
    m_new = jnp.maximum(m, jnp.max(logits, axis=1, keepdims=True))
    alpha = jnp.exp(m - m_new)
    p = jnp.exp(logits - m_new)
    l_new = alpha * l + jnp.sum(p, axis=1, keepdims=True)
    acc_new = alpha * acc + jnp.dot(p.astype(BF16), v, preferred_element_type=F32)
    return m_new, l_new, acc_new


def _stack_heads(q, g, per_group):
    return jnp.concatenate(
        [q[:, (g * per_group + h) * HEAD_DIM:(g * per_group + h + 1) * HEAD_DIM] for h in range(per_group)], axis=0)


def _dsa_prompt_kernel(q_ref, misc_ref, ik_ref, akv_ref, o_ref, key_scr, *, tq, tk, topk, seq):
    i = pl.program_id(1)
    q0 = i * tq
    n_chunks = (q0 + tq + tk - 1) // tk
    row = q0 + lax.broadcasted_iota(I32, (tq, tk), 0)
    lane_col = lax.broadcasted_iota(I32, (tq, tk), 1)
    lane_tile = lax.broadcasted_iota(I32, (tq, LANES), 1)

    iq = q_ref[:, 512:1024]
    w = misc_ref[:, MISC_IW:MISC_IW + IDX_HEADS]

    def score_chunk(c, _):
        k0 = pl.multiple_of(c * tk, tk)
        kc = ik_ref[pl.ds(k0, tk), :].astype(BF16)
        s = jnp.zeros((tq, tk), F32)
        for h in range(IDX_HEADS):
            d = lax.dot_general(iq[:, h * IDX_DIM:(h + 1) * IDX_DIM], kc, _NT, preferred_element_type=F32)
            s = s + jnp.maximum(d, 0.0) * w[:, h:h + 1]
        key = jnp.where(k0 + lane_col <= row, _sortable_key(s), INT_MIN)
        key_scr[c] = key
        return 0

    lax.fori_loop(0, n_chunks, score_chunk, 0)

    def count(pred_fn):
        def body(c, acc):
            k0 = c * tk
            for j in range(tk // LANES):
                sl = slice(j * LANES, (j + 1) * LANES)
                acc = acc + jnp.where(pred_fn(key_scr[c, :, sl], k0 + j * LANES + lane_tile), 1.0, 0.0)
            return acc
        acc = lax.fori_loop(0, n_chunks, body, jnp.zeros((tq, LANES), F32))
        return jnp.broadcast_to(jnp.sum(acc, axis=1, keepdims=True), (tq, LANES))

    def tau_bit(b, tau):
        cand = tau + lax.shift_left(jnp.int32(1), (31 - b).astype(I32))
        cnt = count(lambda k, col: k >= cand)
        return jnp.where(cnt >= topk, cand, tau)

    tau = lax.fori_loop(0, 32, tau_bit, jnp.full((tq, LANES), INT_MIN, I32))

    need = topk - count(lambda k, col: k > tau)
    idx_bits = max(1, int(seq - 1).bit_length())

    def jlim_bit(b, p):
        cand = p + lax.shift_left(jnp.int32(1), (idx_bits - 1 - b).astype(I32))
        cnt = count(lambda k, col: (k == tau) & (col < cand))
        return jnp.where(cnt < need, cand, p)

    jlim = lax.fori_loop(0, idx_bits, jlim_bit, jnp.zeros((tq, LANES), I32))

    tau_c = jnp.broadcast_to(tau[:, 0:1], (tq, tk))
    jlim_c = jnp.broadcast_to(jlim[:, 0:1], (tq, tk))

    aq = q_ref[:, 0:512]
    per_group = A_HEADS // A_KV_HEADS
    qg = [_stack_heads(aq, g, per_group) for g in range(A_KV_HEADS)]

    def attn_chunk(c, carry):
        k0 = pl.multiple_of(c * tk, tk)
        key = key_scr[c]
        col = k0 + lane_col
        sel = ((key > tau_c) | ((key == tau_c) & (col <= jlim_c))) & (col <= row)
        kv = akv_ref[pl.ds(k0, tk), :]
        out = []
        for g in range(A_KV_HEADS):
            k = kv[:, g * HEAD_DIM:(g + 1) * HEAD_DIM].astype(BF16)
            v = kv[:, LANES + g * HEAD_DIM:LANES + (g + 1) * HEAD_DIM].astype(BF16)
            logits = lax.dot_general(qg[g], k, _NT, preferred_element_type=F32)
            logits = jnp.where(sel[None], logits.reshape(per_group, tq, tk), NEG).reshape(per_group * tq, tk)
            out.append(_online_softmax_step(carry[g], logits, v))
        return tuple(out)

    init = tuple((jnp.full((per_group * tq, 1), NEG, F32), jnp.zeros((per_group * tq, 1), F32),
                  jnp.zeros((per_group * tq, HEAD_DIM), F32)) for _ in range(A_KV_HEADS))
    res = lax.fori_loop(0, n_chunks, attn_chunk, init)
    heads = []
    for g in range(A_KV_HEADS):
        _, l, acc = res[g]
        o = acc / l
        heads += [o[h * tq:(h + 1) * tq] for h in range(per_group)]
    o_ref[...] = jnp.concatenate(heads, axis=1).astype(BF16)


def dsa_prompt(q_all, misc, ik, akv, batch, seq, tq, tk):
    n = batch * seq
    nq = seq // tq
    kern = functools.partial(_dsa_prompt_kernel, tq=tq, tk=tk, topk=min(DSA_TOPK, seq // 4), seq=seq)
    qrow = lambda b, i: (b * nq + i, 0)
    return pl.pallas_call(
        kern,
        grid=(batch, nq),
        in_specs=[pl.BlockSpec((tq, AB_Q_W), qrow), pl.BlockSpec((tq, LANES), qrow),
                  pl.BlockSpec((seq, IDX_DIM), lambda b, i: (b, 0)),
                  pl.BlockSpec((seq, 256), lambda b, i: (b, 0))],
        out_specs=pl.BlockSpec((tq, A_HEADS * HEAD_DIM), qrow),
        out_shape=jax.ShapeDtypeStruct((n, A_HEADS * HEAD_DIM), BF16),
        scratch_shapes=[pltpu.VMEM((seq // tk, tq, tk), I32)],
        compiler_params=_cparams(("parallel", "parallel")),
        name="dsa_prompt",
    )(q_all, misc, ik, akv)


def _nsa_prompt_kernel(q_ref, misc_ref, ckv_ref, skv_ref, wkv_ref, o_ref, kc_scr, *, tq, tk, seq, wlen):
    i = pl.program_id(1)
    q0 = i * tq
    nc = seq // CMP_BLOCK
    n_top = min(SEL_TOPN, -(-seq // SEL_BLOCK))
    per_group = B_HEADS // B_KV_HEADS
    n_chunks = (q0 + tq + tk - 1) // tk

    @pl.when(i == 0)
    def _():
        rows = 8 * CMP_BLOCK
        def body(j, _):
            r0 = pl.multiple_of(j * rows, rows)
            x = ckv_ref[pl.ds(r0, rows), :].reshape(8, CMP_BLOCK, 256)
            kc_scr[pl.ds(pl.multiple_of(j * 8, 8), 8), :] = jnp.sum(x, axis=1) * (1.0 / CMP_BLOCK)
            return 0
        lax.fori_loop(0, seq // rows, body, 0)

    bq = q_ref[:, 1024:1536]
    gates = misc_ref[:, MISC_GATE:MISC_GATE + 3 * B_HEADS]
    kc = kc_scr[...]

    blk_t = lax.broadcasted_iota(I32, (nc, tq), 0)
    qpos_t = q0 + lax.broadcasted_iota(I32, (nc, tq), 1)
    visible = (blk_t + 1) * CMP_BLOCK - 1 <= qpos_t
    cur_t = qpos_t // SEL_BLOCK
    o_c, sel_t = [], []
    for g in range(B_KV_HEADS):
        kck = kc[:, g * HEAD_DIM:(g + 1) * HEAD_DIM].astype(BF16)
        kcv = kc[:, LANES + g * HEAD_DIM:LANES + (g + 1) * HEAD_DIM].astype(BF16)
        imp = jnp.zeros((nc, tq), F32)
        for h in range(per_group):
            hh = g * per_group + h
            qh = bq[:, hh * HEAD_DIM:(hh + 1) * HEAD_DIM]
            lc = lax.dot_general(kck, qh, _NT, preferred_element_type=F32)
            lc = jnp.where(visible, lc, -jnp.inf)
            m = jnp.max(lc, axis=0, keepdims=True)
            e = jnp.exp(lc - jnp.where(m > -jnp.inf, m, 0.0))
            ssum = jnp.sum(e, axis=0, keepdims=True)
            pc = e / jnp.where(ssum > 0, ssum, 1.0)
            imp = imp + pc
            o_c.append(lax.dot_general(pc.astype(BF16), kcv, _TN, preferred_element_type=F32))
        imp = jnp.where((blk_t == 0) | (blk_t == cur_t), jnp.inf, imp)
        imp = jnp.where(blk_t > cur_t, -jnp.inf, imp)
        rank = jnp.zeros((nc, tq), F32)
        for j in range(nc):
            r = imp[j:j + 1, :]
            rank = rank + jnp.where((r > imp) | ((r == imp) & (blk_t > j)), 1.0, 0.0)
        sel_t.append(jnp.where(rank < n_top, 1.0, 0.0).astype(BF16))

    qg = [_stack_heads(bq, g, per_group) for g in range(B_KV_HEADS)]
    row = q0 + lax.broadcasted_iota(I32, (tq, tk), 0)
    lane_col = lax.broadcasted_iota(I32, (tq, tk), 1)
    blk_e = lax.broadcasted_iota(I32, (nc, tk), 0)
    col_e = lax.broadcasted_iota(I32, (nc, tk), 1)

    def sel_chunk(c, carry):
        k0 = pl.multiple_of(c * tk, tk)
        expand = jnp.where((k0 + col_e) // SEL_BLOCK == blk_e, 1.0, 0.0).astype(BF16)
        causal = k0 + lane_col <= row
        kv = skv_ref[pl.ds(k0, tk), :]
        out = []
        for g in range(B_KV_HEADS):
            picked = lax.dot_general(sel_t[g], expand, _TN, preferred_element_type=F32) > 0.5
            k = kv[:, g * HEAD_DIM:(g + 1) * HEAD_DIM].astype(BF16)
            v = kv[:, LANES + g * HEAD_DIM:LANES + (g + 1) * HEAD_DIM].astype(BF16)
            logits = lax.dot_general(qg[g], k, _NT, preferred_element_type=F32)
            logits = jnp.where((picked & causal)[None], logits.reshape(per_group, tq, tk), NEG)
            out.append(_online_softmax_step(carry[g], logits.reshape(per_group * tq, tk), v))
        return tuple(out)

    init = tuple((jnp.full((per_group * tq, 1), NEG, F32), jnp.zeros((per_group * tq, 1), F32),
                  jnp.zeros((per_group * tq, HEAD_DIM), F32)) for _ in range(B_KV_HEADS))
    res = lax.fori_loop(0, n_chunks, sel_chunk, init)
    o_s = []
    for g in range(B_KV_HEADS):
        _, l, acc = res[g]
        o = acc / l
        o_s += [o[h * tq:(h + 1) * tq] for h in range(per_group)]

    ws = pl.multiple_of(jnp.maximum(q0 + tq - wlen, 0), tq)
    wkv = wkv_ref[pl.ds(ws, wlen), :]
    dist = (q0 + lax.broadcasted_iota(I32, (tq, wlen), 0)) - (ws + lax.broadcasted_iota(I32, (tq, wlen), 1))
    in_win = (dist >= 0) & (dist <= WINDOW)
    o_w = []
    for g in range(B_KV_HEADS):
        k = wkv[:, g * HEAD_DIM:(g + 1) * HEAD_DIM].astype(BF16)
        v = wkv[:, LANES + g * HEAD_DIM:LANES + (g + 1) * HEAD_DIM].astype(BF16)
        logits = lax.dot_general(qg[g], k, _NT, preferred_element_type=F32)
        logits = jnp.where(in_win[None], logits.reshape(per_group, tq, wlen), NEG).reshape(per_group * tq, wlen)
        m = jnp.max(logits, axis=1, keepdims=True)
        p = jnp.exp(logits - m)
        o = jnp.dot(p.astype(BF16), v, preferred_element_type=F32) / jnp.sum(p, axis=1, keepdims=True)
        o_w += [o[h * tq:(h + 1) * tq] for h in range(per_group)]

    heads = []
    for h in range(B_HEADS):
        gate = lambda br: gates[:, br * B_HEADS + h: br * B_HEADS + h + 1]
        heads.append(gate(0) * o_c[h] + gate(1) * o_s[h] + gate(2) * o_w[h])
    o_ref[...] = jnp.concatenate(heads, axis=1).astype(BF16)


def nsa_prompt(q_all, misc, ckv, skv, wkv, batch, seq, tq, tk):
    n = batch * seq
    nq = seq // tq
    wlen = min(WINDOW + tq, seq)
    kern = functools.partial(_nsa_prompt_kernel, tq=tq, tk=tk, seq=seq, wlen=wlen)
    qrow = lambda b, i: (b * nq + i, 0)
    full = lambda b, i: (b, 0)
    return pl.pallas_call(
        kern,
        grid=(batch, nq),
        in_specs=[pl.BlockSpec((tq, AB_Q_W), qrow), pl.BlockSpec((tq, LANES), qrow),
                  pl.BlockSpec((seq, 256), full), pl.BlockSpec((seq, 256), full), pl.BlockSpec((seq, 256), full)],
        out_specs=pl.BlockSpec((tq, B_HEADS * HEAD_DIM), qrow),
        out_shape=jax.ShapeDtypeStruct((n, B_HEADS * HEAD_DIM), BF16),
        scratch_shapes=[pltpu.VMEM((seq // CMP_BLOCK, 256), F32)],
        compiler_params=_cparams(("parallel", "arbitrary")),
        name="nsa_prompt",
    )(q_all, misc, ckv, skv, wkv)


MLA_CAT = C_HEADS * LANES
MLA_DOWN_COLS = C_Q_RANK + C_KV_RANK + LANES
ROPE_LO = C_NOPE


def _prep_mla(w_down, w_uq, w_uk, w_uv):
    cq, ckv, kpe = jnp.split(w_down, [C_Q_RANK, C_Q_RANK + C_KV_RANK], axis=1)
    z = lambda n: jnp.zeros((w_down.shape[0], n), w_down.dtype)
    w_down_r = jnp.concatenate([cq, ckv, z(ROPE_LO), kpe, z(LANES - ROPE_LO - C_ROPE)], axis=1).astype(BF16)
    uq = w_uq.reshape(C_Q_RANK, C_HEADS, C_NOPE + C_ROPE)
    uq = jnp.pad(uq, ((0, 0), (0, 0), (0, LANES - C_NOPE - C_ROPE))).reshape(C_Q_RANK, MLA_CAT).astype(BF16)
    uk = jnp.pad(w_uk, ((0, 0), (0, 0), (0, LANES - C_NOPE))).reshape(C_KV_RANK, MLA_CAT).astype(BF16)
    uv = w_uv.reshape(C_KV_RANK, C_HEADS * C_VDIM).astype(BF16)
    return w_down_r, uq, uk, uv


def _rms(x, g):
    return x * lax.rsqrt(jnp.mean(x * x, axis=-1, keepdims=True) + RMS_EPS) * g


def _mla_proj_kernel(x_ref, wd_ref, uq_ref, uk_ref, uv_ref, gq_ref, gkv_ref, c_ref, s_ref,
                     qcat_ref, kcat_ref, v_ref, ckv_ref, kpe_ref):
    xb = x_ref[...].astype(BF16)
    tab_c, tab_s = c_ref[...], s_ref[...]
    lane = lax.broadcasted_iota(I32, tab_c.shape, 1)
    first = (lane % C_ROPE) < C_ROPE // 2
    rope = lambda v: _swap_rope(v, tab_c, tab_s, first, C_ROPE // 2)

    c = jnp.dot(xb, wd_ref[...], preferred_element_type=F32)
    cq = _rms(c[:, 0:C_Q_RANK], gq_ref[...]).astype(BF16)
    ckv = _rms(c[:, C_Q_RANK:C_Q_RANK + C_KV_RANK], gkv_ref[...])
    kpe = rope(c[:, C_Q_RANK + C_KV_RANK:])
    ckv_ref[...] = ckv
    kpe_ref[...] = kpe[:, ROPE_LO:ROPE_LO + C_ROPE]
    ckv_b = ckv.astype(BF16)
    v_ref[...] = jnp.dot(ckv_b, uv_ref[...], preferred_element_type=F32).astype(BF16)
    for h in range(C_HEADS):
        sl = slice(h * LANES, (h + 1) * LANES)
        q = jnp.dot(cq, uq_ref[:, sl], preferred_element_type=F32)
        qcat_ref[:, sl] = rope(q).astype(BF16)
        k = jnp.dot(ckv_b, uk_ref[:, sl], preferred_element_type=F32)
        kcat_ref[:, sl] = (k + kpe).astype(BF16)


def mla_proj(x, w_down_r, uq, uk, uv, g_q, g_kv, tab_c, tab_s, tm):
    n, d = x.shape
    n_tab = tab_c.shape[0] // tm
    row = lambda i: (i, 0)
    tab = lambda i: (i % n_tab, 0)
    out_shape = (
        jax.ShapeDtypeStruct((n, MLA_CAT), BF16), jax.ShapeDtypeStruct((n, MLA_CAT), BF16),
        jax.ShapeDtypeStruct((n, C_HEADS * C_VDIM), BF16),
        jax.ShapeDtypeStruct((n, C_KV_RANK), F32), jax.ShapeDtypeStruct((n, C_ROPE), F32),
    )
    out_specs = (
        pl.BlockSpec((tm, MLA_CAT), row), pl.BlockSpec((tm, MLA_CAT), row),
        pl.BlockSpec((tm, C_HEADS * C_VDIM), row),
        pl.BlockSpec((tm, C_KV_RANK), row), pl.BlockSpec((tm, C_ROPE), row),
    )
    return pl.pallas_call(
        _mla_proj_kernel,
        grid=(n // tm,),
        in_specs=[pl.BlockSpec((tm, d), row), _resident(w_down_r.shape), _resident(uq.shape),
                  _resident(uk.shape), _resident(uv.shape),
                  _resident((1, C_Q_RANK)), _resident((1, C_KV_RANK)),
                  pl.BlockSpec((tm, LANES), tab), pl.BlockSpec((tm, LANES), tab)],
        out_specs=out_specs,
        out_shape=out_shape,
        compiler_params=_cparams(("parallel",)),
        name="mla_proj",
    )(x, w_down_r, uq, uk, uv, g_q.reshape(1, -1), g_kv.reshape(1, -1), tab_c, tab_s)


def _mla_attn_kernel(q_ref, k_ref, v_ref, o_ref, m_scr, l_scr, acc_scr, *, tq, tk):
    i, j = pl.program_id(1), pl.program_id(2)
    scale = (C_NOPE + C_ROPE) ** -0.5

    @pl.when(j == 0)
    def _():
        m_scr[...] = jnp.full(m_scr.shape, NEG, F32)
        l_scr[...] = jnp.zeros(l_scr.shape, F32)
        acc_scr[...] = jnp.zeros(acc_scr.shape, F32)

    @pl.when(j * tk <= i * tq + tq - 1)
    def _():
        row = i * tq + lax.broadcasted_iota(I32, (tq, tk), 0)
        col = j * tk + lax.broadcasted_iota(I32, (tq, tk), 1)
        causal = col <= row
        for h in range(C_HEADS):
            q = q_ref[:, h * LANES:(h + 1) * LANES]
            k = k_ref[:, h * LANES:(h + 1) * LANES]
            v = v_ref[:, h * C_VDIM:(h + 1) * C_VDIM]
            logits = lax.dot_general(q, k, _NT, preferred_element_type=F32) * scale
            logits = jnp.where(causal, logits, NEG)
            m, l, acc = _online_softmax_step((m_scr[h], l_scr[h], acc_scr[h]), logits, v)
            m_scr[h], l_scr[h], acc_scr[h] = m, l, acc

    @pl.when(j == pl.num_programs(2) - 1)
    def _():
        o_ref[...] = jnp.concatenate([acc_scr[h] / l_scr[h] for h in range(C_HEADS)], axis=1).astype(BF16)


def mla_attn(qcat, kcat, v, batch, seq, tq, tk):
    n = batch * seq
    nq, nk = seq // tq, seq // tk
    kern = functools.partial(_mla_attn_kernel, tq=tq, tk=tk)
    last = lambda i: (i * tq + tq - 1) // tk
    kv_row = lambda b, i, j: (b * nk + jnp.minimum(j, last(i)), 0)
    return pl.pallas_call(
        kern,
        grid=(batch, nq, nk),
        in_specs=[pl.BlockSpec((tq, MLA_CAT), lambda b, i, j: (b * nq + i, 0)),
                  pl.BlockSpec((tk, MLA_CAT), kv_row),
                  pl.BlockSpec((tk, C_HEADS * C_VDIM), kv_row)],
        out_specs=pl.BlockSpec((tq, C_HEADS * C_VDIM), lambda b, i, j: (b * nq + i, 0)),
        out_shape=jax.ShapeDtypeStruct((n, C_HEADS * C_VDIM), BF16),
        scratch_shapes=[pltpu.VMEM((C_HEADS, tq, 1), F32), pltpu.VMEM((C_HEADS, tq, 1), F32),
                        pltpu.VMEM((C_HEADS, tq, C_VDIM), F32)],
        compiler_params=_cparams(("parallel", "parallel", "arbitrary")),
        name="mla_attn",
    )(qcat, kcat, v)


def _jx_layer_norm(x, g, b):
    mu = jnp.mean(x, -1, keepdims=True)
    var = jnp.mean(jnp.square(x - mu), -1, keepdims=True)
    return (x - mu) * lax.rsqrt(var + LN_EPS) * g + b


def _jx_rope(x, pos):
    d = x.shape[-1]
    inv = jnp.power(ROPE_THETA, -jnp.arange(0, d, 2, dtype=F32) / d)
    ang = pos.astype(F32)[:, None] * inv[None, :]
    shape = (pos.shape[0],) + (1,) * (x.ndim - 3) + (d // 2,)
    cos, sin = jnp.cos(ang).reshape(shape), jnp.sin(ang).reshape(shape)
    x1, x2 = jnp.split(x, 2, axis=-1)
    return jnp.concatenate([x1 * cos - x2 * sin, x2 * cos + x1 * sin], -1)


def _jx_masked_softmax(logits, mask):
    logits = jnp.where(mask, logits, -jnp.inf)
    m = jnp.max(logits, -1, keepdims=True)
    e = jnp.exp(logits - jnp.where(jnp.isfinite(m), m, 0.0))
    s = jnp.sum(e, -1, keepdims=True)
    return e / jnp.where(s > 0, s, 1.0)


def _jx_gather_past(pool, page_table):
    b, n_pages = page_table.shape
    return pool[page_table].reshape((b, n_pages * PAGE_SIZE) + pool.shape[2:])


def _jx_fetch_paged(pool, page_table, new_rows, pos, head=None):
    past = page_table.shape[1] * PAGE_SIZE
    bidx = jnp.arange(pos.shape[0]).reshape((-1,) + (1,) * (pos.ndim - 1))
    pp = jnp.clip(pos, 0, past - 1)
    page = page_table[bidx, pp // PAGE_SIZE]
    off = pp % PAGE_SIZE
    pn = jnp.clip(pos - past, 0, new_rows.shape[1] - 1)
    if head is None:
        old, new = pool[page, off], new_rows[bidx, pn]
    else:
        old, new = pool[page, off, :, head], new_rows[bidx, pn, :, head]
    in_past = (pos < past).reshape(pos.shape + (1,) * (old.ndim - pos.ndim))
    return jnp.where(in_past, old, new)


def _jx_block_means(rows):
    b, l = rows.shape[:2]
    nc = l // CMP_BLOCK
    return rows[:, :nc * CMP_BLOCK].reshape((b, nc, CMP_BLOCK) + rows.shape[2:]).mean(2)


def _jx_gathered_attend(q, k, v, mask):
    b, t, h, d = q.shape
    kvh = k.shape[2]
    qg = q.reshape(b, t, kvh, h // kvh, d)
    logits = jnp.einsum('bthgd,bthnd->bthgn', qg, k) * d ** -0.5
    p = _jx_masked_softmax(logits, mask[:, :, :, None, :])
    return jnp.einsum('bthgn,bthnd->bthgd', p, v).reshape(b, t, h, d)


_AB_WIDTHS = (512, 128, 128, 512, 64, 8, 512, 128, 128, 128, 128, 128, 128, 24)
_AB_SPLITS = tuple(int(v) for v in np.cumsum(_AB_WIDTHS)[:-1])


def _jx_ab_project(x, pos, w_in):
    b, t, _ = x.shape
    (aq, ak, av, iq, ik, iw, bq, ck, cv, sk, sv, wk, wv, bg) = jnp.split(x @ w_in, _AB_SPLITS, axis=-1)
    heads = lambda a, n, d: a.reshape(b, t, n, d)
    kv_pair = lambda k_, v_, n: jnp.stack([_jx_rope(heads(k_, n, HEAD_DIM), pos), heads(v_, n, HEAD_DIM)], axis=2)
    return dict(
        aq=_jx_rope(heads(aq, A_HEADS, HEAD_DIM), pos), a_kv=kv_pair(ak, av, A_KV_HEADS),
        iq=_jx_rope(heads(iq, IDX_HEADS, IDX_DIM), pos), ik=_jx_rope(ik, pos), iw=iw * IDX_HEADS ** -0.5,
        bq=_jx_rope(heads(bq, B_HEADS, HEAD_DIM), pos), cmp_kv=kv_pair(ck, cv, B_KV_HEADS),
        sel_kv=kv_pair(sk, sv, B_KV_HEADS), win_kv=kv_pair(wk, wv, B_KV_HEADS),
        gates=jax.nn.sigmoid(bg).reshape(b, t, 3, B_HEADS))


def _jx_ab_mixer_sample(x, cache_dsa_kv, cache_dsa_idx, cache_nsa_cmp_kv, cache_nsa_sel_kv, state_nsa_win_kv,
                        page_table, w_in, w_out):
    b, t, _ = x.shape
    past = page_table.shape[1] * PAGE_SIZE
    qpos = past + jnp.arange(t)
    n_keys = past + t
    p = _jx_ab_project(x, qpos, w_in)
    scores = []
    for ik, kpos in [(_jx_gather_past(cache_dsa_idx, page_table), jnp.arange(past)), (p['ik'], qpos)]:
        dots = jnp.einsum('bthd,bsd->bths', p['iq'], ik) * IDX_DIM ** -0.5
        s = jnp.einsum('bths,bth->bts', jax.nn.relu(dots), p['iw'])
        scores.append(jnp.where(kpos[None, None, :] <= qpos[None, :, None], s, -jnp.inf))
    idx = lax.top_k(jnp.concatenate(scores, -1), min(DSA_TOPK, n_keys // 4))[1]
    rows = _jx_fetch_paged(cache_dsa_kv, page_table, p['a_kv'], idx)
    k = jnp.swapaxes(rows[..., 0, :, :], 2, 3)
    v = jnp.swapaxes(rows[..., 1, :, :], 2, 3)
    o_a = _jx_gathered_attend(p['aq'], k, v, (idx <= qpos[None, :, None])[:, :, None, :])
    kc = jnp.concatenate([_jx_block_means(_jx_gather_past(cache_nsa_cmp_kv, page_table)),
                          _jx_block_means(p['cmp_kv'])], axis=1)
    nbuf = state_nsa_win_kv.shape[1]
    win = jnp.concatenate([state_nsa_win_kv, p['win_kv']], axis=1)
    wpos = jnp.concatenate([past - nbuf + jnp.arange(nbuf), qpos])
    q, gates = p['bq'], p['gates']
    h, d = B_HEADS, HEAD_DIM
    g = h // B_KV_HEADS
    qg = q.reshape(b, t, B_KV_HEADS, g, d)
    scale = d ** -0.5
    nc = kc.shape[1]
    c_end = (jnp.arange(nc) + 1) * CMP_BLOCK - 1
    lc = jnp.einsum('bthgd,bchd->bthgc', qg, kc[:, :, 0]) * scale
    pc = _jx_masked_softmax(lc, (c_end[None, :] <= qpos[:, None])[None, :, None, None, :])
    o_c = jnp.einsum('bthgc,bchd->bthgd', pc, kc[:, :, 1]).reshape(b, t, h, d)
    n_blk = -(-n_keys // SEL_BLOCK)
    blk = jnp.arange(n_blk)
    cur = (qpos // SEL_BLOCK)[:, None, None]
    imp = jnp.pad(pc.sum(3), ((0, 0), (0, 0), (0, 0), (0, n_blk - nc)))
    imp = jnp.where((blk == 0) | (blk == cur), jnp.inf, imp)
    imp = jnp.where(blk > cur, -jnp.inf, imp)
    n_top = min(SEL_TOPN, n_blk)
    sel = lax.top_k(imp, n_top)[1]
    spos = (sel[..., None] * SEL_BLOCK + jnp.arange(SEL_BLOCK)).reshape(b, t, B_KV_HEADS, n_top * SEL_BLOCK)
    rows = _jx_fetch_paged(cache_nsa_sel_kv, page_table, p['sel_kv'], spos,
                           jnp.arange(B_KV_HEADS)[None, None, :, None])
    o_s = _jx_gathered_attend(q, rows[..., 0, :], rows[..., 1, :], spos <= qpos[None, :, None, None])
    lw = jnp.einsum('bthgd,bshd->bthgs', qg, win[:, :, 0]) * scale
    dist = qpos[:, None] - wpos[None, :]
    mw = (dist >= 0) & (dist <= WINDOW) & (wpos >= 0)[None, :]
    pw = _jx_masked_softmax(lw, mw[None, :, None, None, :])
    o_w = jnp.einsum('bthgs,bshd->bthgd', pw, win[:, :, 1]).reshape(b, t, h, d)
    gt = gates[..., None]
    o_b = gt[:, :, 0] * o_c + gt[:, :, 1] * o_s + gt[:, :, 2] * o_w
    y = jnp.concatenate([o_a.reshape(b, t, -1), o_b.reshape(b, t, -1)], -1) @ w_out
    return y, (p['a_kv'], p['ik'], p['cmp_kv'], p['sel_kv'], win[:, -nbuf:])


def _jx_rms_norm(x, g):
    return x * lax.rsqrt(jnp.mean(x * x, -1, keepdims=True) + RMS_EPS) * g


def _jx_mla_sample(x, cache_mla_ckv, cache_mla_kpe, page_table, w_down, g_q, g_kv, w_uq, w_uk, w_uv, w_out):
    b, t, _ = x.shape
    past = page_table.shape[1] * PAGE_SIZE
    qpos = past + jnp.arange(t)
    cq, ckv, kpe = jnp.split(x @ w_down, [C_Q_RANK, C_Q_RANK + C_KV_RANK], axis=-1)
    q = (_jx_rms_norm(cq, g_q) @ w_uq).reshape(b, t, C_HEADS, C_NOPE + C_ROPE)
    q_nope, q_pe = jnp.split(q, [C_NOPE], axis=-1)
    q_pe, ckv, kpe = _jx_rope(q_pe, qpos), _jx_rms_norm(ckv, g_kv), _jx_rope(kpe, qpos)
    q_lat = jnp.einsum('bthd,chd->bthc', q_nope, w_uk)
    ckv_past = _jx_gather_past(cache_mla_ckv, page_table)
    kpe_past = _jx_gather_past(cache_mla_kpe, page_table)
    scale = (C_NOPE + C_ROPE) ** -0.5
    seg = lambda c, r: (jnp.einsum('bthc,bsc->bths', q_lat, c) + jnp.einsum('bthr,bsr->bths', q_pe, r)) * scale
    mask = jnp.concatenate([jnp.ones((t, past), bool), qpos[None, :] <= qpos[:, None]], -1)
    pr = _jx_masked_softmax(jnp.concatenate([seg(ckv_past, kpe_past), seg(ckv, kpe)], -1), mask[None, :, None, :])
    o_lat = (jnp.einsum('bths,bsc->bthc', pr[..., :past], ckv_past)
             + jnp.einsum('bths,bsc->bthc', pr[..., past:], ckv))
    o = jnp.einsum('bthc,chd->bthd', o_lat, w_uv)
    return o.reshape(b, t, -1) @ w_out, (ckv, kpe)


def _jx_half_ffn(x, w_gate, w_up, w_down):
    return 0.5 * ((jax.nn.silu(x @ w_gate) * (x @ w_up)) @ w_down)


TM = 512
TQ_SPARSE = 128
TK_SPARSE = 512
TQ_MLA, TK_MLA = 256, 512


def kernel(x_prompt, x_sample, cache_dsa_kv, cache_dsa_idx, cache_nsa_cmp_kv, cache_nsa_sel_kv, state_nsa_win_kv,
           cache_mla_ckv, cache_mla_kpe, page_table, ln_g, ln_b, ffn_w_gate, ffn_w_up, ffn_w_down, ab_w_in,
           ab_w_out, mla_w_down, mla_g_q, mla_g_kv, mla_w_uq, mla_w_uk, mla_w_uv, mla_w_out):
    batch, seq, d = x_prompt.shape
    n = batch * seq
    tm = min(TM, n)
    tq_s, tk_s = min(TQ_SPARSE, seq), min(TK_SPARSE, seq)
    tq_m, tk_m = min(TQ_MLA, seq), min(TK_MLA, seq)
    pos = jnp.arange(seq)

    yp = x_prompt.reshape(n, d)
    ys = x_sample
    ffn_w = [[_prep_ffn(ffn_w_gate[l, j], ffn_w_up[l, j], ffn_w_down[l, j]) for j in range(2)] for l in range(DEPTH)]

    g, b = ln_g[0], ln_b[0]
    yp = ffn_ln(yp, *ffn_w[0][0], g[0], b[0], tm)
    ys = _jx_layer_norm(DN_ALPHA * ys + _jx_half_ffn(ys, ffn_w_gate[0, 0], ffn_w_up[0, 0], ffn_w_down[0, 0]), g[0], b[0])

    tab_c, tab_s = _rope_tables(pos, HEAD_DIM, 0, LANES)
    q_all, akv, ckv, skv, wkv, ik, misc = ab_proj(yp, _prep_ab_w_in(ab_w_in), tab_c, tab_s, tm)
    o_a = dsa_prompt(q_all, misc, ik, akv, batch, seq, tq_s, tk_s)
    o_b = nsa_prompt(q_all, misc, ckv, skv, wkv, batch, seq, tq_s, tk_s)
    w_out_b = ab_w_out.astype(BF16)
    half = A_HEADS * HEAD_DIM
    yp = proj_ln(yp, [o_a, o_b], [w_out_b[:half], w_out_b[half:]], g[1], b[1], tm)
    ms, (dsa_kv_s, dsa_idx_s, cmp_kv_s, sel_kv_s, win_kv_s) = _jx_ab_mixer_sample(
        ys, cache_dsa_kv, cache_dsa_idx, cache_nsa_cmp_kv, cache_nsa_sel_kv, state_nsa_win_kv,
        page_table, ab_w_in, ab_w_out)
    ys = _jx_layer_norm(DN_ALPHA * ys + ms, g[1], b[1])

    yp = ffn_ln(yp, *ffn_w[0][1], g[2], b[2], tm)
    ys = _jx_layer_norm(DN_ALPHA * ys + _jx_half_ffn(ys, ffn_w_gate[0, 1], ffn_w_up[0, 1], ffn_w_down[0, 1]), g[2], b[2])

    kv5 = lambda a: a.reshape(batch, seq, 2, 2, HEAD_DIM)
    dsa_kv_p, cmp_kv_p, sel_kv_p = kv5(akv), kv5(ckv), kv5(skv)
    win_kv_p = kv5(wkv)[:, -min(WINDOW, seq):]
    dsa_idx_p = ik.reshape(batch, seq, IDX_DIM)

    g, b = ln_g[1], ln_b[1]
    yp = ffn_ln(yp, *ffn_w[1][0], g[0], b[0], tm)
    ys = _jx_layer_norm(DN_ALPHA * ys + _jx_half_ffn(ys, ffn_w_gate[1, 0], ffn_w_up[1, 0], ffn_w_down[1, 0]), g[0], b[0])

    mtab_c, mtab_s = _rope_tables(pos, C_ROPE, ROPE_LO, ROPE_LO + C_ROPE)
    w_down_r, uq, uk, uv = _prep_mla(mla_w_down, mla_w_uq, mla_w_uk, mla_w_uv)
    qcat, kcat, v, ckv_p, kpe_p = mla_proj(yp, w_down_r, uq, uk, uv, mla_g_q, mla_g_kv, mtab_c, mtab_s, tm)
    o_m = mla_attn(qcat, kcat, v, batch, seq, tq_m, tk_m)
    yp = proj_ln(yp, [o_m], [mla_w_out.astype(BF16)], g[1], b[1], tm)
    ms, (ckv_s, kpe_s) = _jx_mla_sample(ys, cache_mla_ckv, cache_mla_kpe, page_table, mla_w_down, mla_g_q,
                                        mla_g_kv, mla_w_uq, mla_w_uk, mla_w_uv, mla_w_out)
    ys = _jx_layer_norm(DN_ALPHA * ys + ms, g[1], b[1])

    yp = ffn_ln(yp, *ffn_w[1][1], g[2], b[2], tm)
    ys = _jx_layer_norm(DN_ALPHA * ys + _jx_half_ffn(ys, ffn_w_gate[1, 1], ffn_w_up[1, 1], ffn_w_down[1, 1]), g[2], b[2])

    return (yp.reshape(batch, seq, d), ys, dsa_kv_p, dsa_kv_s, dsa_idx_p, dsa_idx_s, cmp_kv_p, cmp_kv_s,
            sel_kv_p, sel_kv_s, win_kv_p, win_kv_s, ckv_p.reshape(batch, seq, C_KV_RANK), ckv_s,
            kpe_p.reshape(batch, seq, C_ROPE), kpe_s)
```

```python
import functools

import jax
import jax.numpy as jnp
import numpy as np
from jax import lax
from jax.experimental import pallas as pl
from jax.experimental.pallas import tpu as pltpu

F32 = jnp.float32
BF16 = jnp.bfloat16
I32 = jnp.int32

HEAD_DIM = 64
ROPE_THETA = 10000.0
PAGE_SIZE = 128
A_HEADS, A_KV_HEADS = 8, 2
IDX_HEADS, IDX_DIM = 8, 64
DSA_TOPK = 256
B_HEADS, B_KV_HEADS = 8, 2
CMP_BLOCK = 64
SEL_BLOCK = 64
SEL_TOPN = 16
WINDOW = 512
C_HEADS, C_NOPE, C_ROPE, C_VDIM = 16, 64, 32, 64
C_Q_RANK, C_KV_RANK = 384, 256
LN_EPS = 1e-5
RMS_EPS = 1e-6
DEPTH = 2
DN_ALPHA = (2 * DEPTH) ** 0.25

LANES = 128
VMEM_LIMIT = 56 * 1024 * 1024
NEG = -1e30
INT_MIN = -2 ** 31

_NT = (((1,), (1,)), ((), ()))
_TN = (((0,), (0,)), ((), ()))


def _cparams(sem):
    return pltpu.CompilerParams(dimension_semantics=sem, vmem_limit_bytes=VMEM_LIMIT)


def _resident(shape):
    nd = len(shape)
    return pl.BlockSpec(shape, lambda *_: (0,) * nd, pipeline_mode=pl.Buffered(1))


def _layer_norm(y, g, b):
    mu = jnp.mean(y, axis=-1, keepdims=True)
    yc = y - mu
    var = jnp.mean(yc * yc, axis=-1, keepdims=True)
    return yc * lax.rsqrt(var + LN_EPS) * g + b


def _swap_rope(v, tab_c, tab_s, first, d_half):
    partner = jnp.where(first, pltpu.roll(v, LANES - d_half, 1), pltpu.roll(v, d_half, 1))
    return v * tab_c + partner * tab_s


def _ffn_ln_kernel(x_ref, wg_ref, wu_ref, wd_ref, g_ref, b_ref, o_ref, *, n_chunks, chunk):
    x = x_ref[...]
    xb = x.astype(BF16)
    acc = None
    for c in range(n_chunks):
        sl = slice(c * chunk, (c + 1) * chunk)
        gate = jnp.dot(xb, wg_ref[:, sl], preferred_element_type=F32)
        up = jnp.dot(xb, wu_ref[:, sl], preferred_element_type=F32)
        h = (gate * jax.nn.sigmoid(gate) * up).astype(BF16)
        part = jnp.dot(h, wd_ref[sl, :], preferred_element_type=F32)
        acc = part if acc is None else acc + part
    y = DN_ALPHA * x + 0.5 * acc
    o_ref[...] = _layer_norm(y, g_ref[...], b_ref[...])


def _prep_ffn(w_gate, w_up, w_down):
    f = w_gate.shape[1]
    fp = -(-f // (2 * LANES)) * (2 * LANES)
    pad = fp - f
    wg = jnp.pad(w_gate, ((0, 0), (0, pad))).astype(BF16)
    wu = jnp.pad(w_up, ((0, 0), (0, pad))).astype(BF16)
    wd = jnp.pad(w_down, ((0, pad), (0, 0))).astype(BF16)
    return wg, wu, wd


def ffn_ln(x, wg, wu, wd, g, b, tm):
    n, d = x.shape
    fp = wg.shape[1]
    n_chunks = 2
    kern = functools.partial(_ffn_ln_kernel, n_chunks=n_chunks, chunk=fp // n_chunks)
    return pl.pallas_call(
        kern,
        grid=(n // tm,),
        in_specs=[
            pl.BlockSpec((tm, d), lambda i: (i, 0)),
            _resident(wg.shape), _resident(wu.shape), _resident(wd.shape),
            _resident((1, d)), _resident((1, d)),
        ],
        out_specs=pl.BlockSpec((tm, d), lambda i: (i, 0)),
        out_shape=jax.ShapeDtypeStruct((n, d), F32),
        compiler_params=_cparams(("parallel",)),
        name="ffn_ln",
    )(x, wg, wu, wd, g.reshape(1, d), b.reshape(1, d))


def _proj_ln_kernel(*refs, n_in):
    x_ref = refs[0]
    o_refs = refs[1:1 + n_in]
    w_refs = refs[1 + n_in:1 + 2 * n_in]
    g_ref, b_ref, y_ref = refs[1 + 2 * n_in:]
    acc = None
    for o_ref, w_ref in zip(o_refs, w_refs):
        part = jnp.dot(o_ref[...], w_ref[...], preferred_element_type=F32)
        acc = part if acc is None else acc + part
    y_ref[...] = _layer_norm(DN_ALPHA * x_ref[...] + acc, g_ref[...], b_ref[...])


def proj_ln(x, outs, ws, g, b, tm):
    n, d = x.shape
    n_in = len(outs)
    kern = functools.partial(_proj_ln_kernel, n_in=n_in)
    in_specs = [pl.BlockSpec((tm, d), lambda i: (i, 0))]
    in_specs += [pl.BlockSpec((tm, o.shape[1]), lambda i: (i, 0)) for o in outs]
    in_specs += [_resident(w.shape) for w in ws]
    in_specs += [_resident((1, d)), _resident((1, d))]
    return pl.pallas_call(
        kern,
        grid=(n // tm,),
        in_specs=in_specs,
        out_specs=pl.BlockSpec((tm, d), lambda i: (i, 0)),
        out_shape=jax.ShapeDtypeStruct((n, d), F32),
        compiler_params=_cparams(("parallel",)),
        name="proj_ln",
    )(x, *outs, *ws, g.reshape(1, d), b.reshape(1, d))


AB_Q_W = 3 * 512
AB_COLS = AB_Q_W + 4 * 256 + 128 + 128
MISC_IW = 0
MISC_GATE = IDX_HEADS


def _ab_column_perm():
    widths = (512, 128, 128, 512, 64, 8, 512, 128, 128, 128, 128, 128, 128, 24)
    offs = np.concatenate([[0], np.cumsum(widths)])
    seg = lambda i: np.arange(offs[i], offs[i + 1])
    (aq, ak, av, iq, ik, iw, bq, ck, cv, sk, sv, wk, wv, bg) = [seg(i) for i in range(14)]
    pad = lambda n: -np.ones(n, np.int64)
    return np.concatenate([aq, iq, bq, ak, av, ck, cv, sk, sv, wk, wv, ik, pad(64), iw, bg, pad(96)])


def _prep_ab_w_in(w_in):
    perm = _ab_column_perm()
    w = jnp.where((perm >= 0)[None, :], jnp.take(w_in, np.maximum(perm, 0), axis=1), 0.0)
    return w.astype(BF16)


def _rope_tables(pos, d, lane_lo, lane_hi):
    inv = jnp.power(ROPE_THETA, -jnp.arange(0, d, 2, dtype=F32) / d)
    ang = pos.astype(F32)[:, None] * inv[None, :]
    cos, sin = jnp.cos(ang), jnp.sin(ang)
    lane = np.arange(LANES)
    inside = (lane >= lane_lo) & (lane < lane_hi)
    j = (lane - lane_lo) % d
    idx = j % (d // 2)
    sign = np.where(j < d // 2, -1.0, 1.0).astype(np.float32)
    tab_c = jnp.where(inside[None, :], cos[:, idx], 1.0)
    tab_s = jnp.where(inside[None, :], sin[:, idx] * sign[None, :], 0.0)
    return tab_c, tab_s


def _ab_proj_kernel(x_ref, w_ref, c_ref, s_ref, q_ref, akv_ref, ckv_ref, skv_ref, wkv_ref, ik_ref, misc_ref):
    xb = x_ref[...].astype(BF16)
    tab_c, tab_s = c_ref[...], s_ref[...]
    lane = lax.broadcasted_iota(I32, tab_c.shape, 1)
    first = (lane % HEAD_DIM) < HEAD_DIM // 2
    rope = lambda v: _swap_rope(v, tab_c, tab_s, first, HEAD_DIM // 2)

    for blk in range(3):
        v = jnp.dot(xb, w_ref[:, blk * 512:(blk + 1) * 512], preferred_element_type=F32)
        for j in range(4):
            q_ref[:, blk * 512 + j * LANES: blk * 512 + (j + 1) * LANES] = (
                rope(v[:, j * LANES:(j + 1) * LANES]) * HEAD_DIM ** -0.5).astype(BF16)
    for n, kv_ref in enumerate((akv_ref, ckv_ref, skv_ref, wkv_ref)):
        c0 = AB_Q_W + n * 256
        v = jnp.dot(xb, w_ref[:, c0:c0 + 256], preferred_element_type=F32)
        kv_ref[:, 0:LANES] = rope(v[:, 0:LANES])
        kv_ref[:, LANES:2 * LANES] = v[:, LANES:2 * LANES]
    c0 = AB_Q_W + 4 * 256
    v = jnp.dot(xb, w_ref[:, c0:c0 + 256], preferred_element_type=F32)
    ik_ref[...] = rope(v[:, 0:LANES])[:, 0:IDX_DIM]
    m = v[:, LANES:2 * LANES]
    mlane = lax.broadcasted_iota(I32, m.shape, 1)
    misc_ref[...] = jnp.where(mlane < IDX_HEADS, m * IDX_HEADS ** -0.5, jax.nn.sigmoid(m))


def ab_proj(x, w_in_b, tab_c, tab_s, tm):
    n, d = x.shape
    n_tab = tab_c.shape[0] // tm
    row = lambda i: (i, 0)
    tab = lambda i: (i % n_tab, 0)
    out_shape = (
        jax.ShapeDtypeStruct((n, AB_Q_W), BF16),
        jax.ShapeDtypeStruct((n, 256), F32), jax.ShapeDtypeStruct((n, 256), F32),
        jax.ShapeDtypeStruct((n, 256), F32), jax.ShapeDtypeStruct((n, 256), F32),
        jax.ShapeDtypeStruct((n, IDX_DIM), F32),
        jax.ShapeDtypeStruct((n, LANES), F32),
    )
    out_specs = (
        pl.BlockSpec((tm, AB_Q_W), row),
        pl.BlockSpec((tm, 256), row), pl.BlockSpec((tm, 256), row),
        pl.BlockSpec((tm, 256), row), pl.BlockSpec((tm, 256), row),
        pl.BlockSpec((tm, IDX_DIM), row),
        pl.BlockSpec((tm, LANES), row),
    )
    return pl.pallas_call(
        _ab_proj_kernel,
        grid=(n // tm,),
        in_specs=[pl.BlockSpec((tm, d), row), _resident(w_in_b.shape),
                  pl.BlockSpec((tm, LANES), tab), pl.BlockSpec((tm, LANES), tab)],
        out_specs=out_specs,
        out_shape=out_shape,
        compiler_params=_cparams(("parallel",)),
        name="ab_proj",
    )(x, w_in_b, tab_c, tab_s)


def _sortable_key(s):
    bits = pltpu.bitcast(s, I32)
    return bits ^ ((bits >> 31) & 0x7FFFFFFF)


def _kth_threshold(count, k, shape, idx_bits):
    def tau_bit(b, tau):
        cand = tau + lax.shift_left(jnp.int32(1), jnp.asarray(31 - b, I32))
        return jnp.where(count(lambda key, idx: key >= cand) >= k, cand, tau)

    tau = lax.fori_loop(0, 32, tau_bit, jnp.full(shape, INT_MIN, I32))
    need = k - count(lambda key, idx: key > tau)

    def jlim_bit(b, p):
        cand = p + lax.shift_left(jnp.int32(1), jnp.asarray(idx_bits - 1 - b, I32))
        return jnp.where(count(lambda key, idx: (key == tau) & (idx < cand)) < need, cand, p)

    tied = count(lambda key, idx: key == tau)
    split = jnp.max(jnp.where(tied > need, 1.0, 0.0)) > 0.0
    jlim = lax.cond(split,
                    lambda: lax.fori_loop(0, idx_bits, jlim_bit, jnp.zeros(shape, I32)),
                    lambda: jnp.full(shape, (1 << idx_bits) - 1, I32))
    return tau, jlim


def _online_softmax_step(carry, logits, v):
    m, l, acc = carry
    m_new = jnp.maximum(m, jnp.max(logits, axis=1, keepdims=True))
    alpha = jnp.exp(m - m_new)
    p = jnp.exp(logits - _lanes_to(m_new, logits.shape[1]))
    l_new = alpha * l + jnp.sum(p, axis=1, keepdims=True)
    acc_new = _lanes_to(alpha, acc.shape[1]) * acc + jnp.dot(p.astype(BF16), v, preferred_element_type=F32)
    return m_new, l_new, acc_new


def _lanes_to(x, width):
    return x[:, 0:width] if width <= LANES else jnp.concatenate([x] * (width // LANES), axis=1)


def _softmax_init(rows, dv):
    return jnp.full((rows, LANES), NEG, F32), jnp.zeros((rows, LANES), F32), jnp.zeros((rows, dv), F32)


def _stack_heads(q, g, per_group):
    return jnp.concatenate(
        [q[:, (g * per_group + h) * HEAD_DIM:(g * per_group + h + 1) * HEAD_DIM] for h in range(per_group)], axis=0)


def _dsa_prompt_kernel(q_ref, misc_ref, ik_ref, akv_ref, o_ref, key_scr, *, tq, tk, topk, seq):
    i = pl.program_id(1)
    q0 = i * tq
    n_chunks = (q0 + tq + tk - 1) // tk
    row = q0 + lax.broadcasted_iota(I32, (tq, tk), 0)
    lane_col = lax.broadcasted_iota(I32, (tq, tk), 1)
    lane_tile = lax.broadcasted_iota(I32, (tq, LANES), 1)

    iq = q_ref[:, 512:1024]
    w = misc_ref[:, MISC_IW:MISC_IW + IDX_HEADS]

    def score_chunk(c, _):
        k0 = pl.multiple_of(c * tk, tk)
        kc = ik_ref[pl.ds(k0, tk), :].astype(BF16)
        s = jnp.zeros((tq, tk), F32)
        for h in range(IDX_HEADS):
            d = lax.dot_general(iq[:, h * IDX_DIM:(h + 1) * IDX_DIM], kc, _NT, preferred_element_type=F32)
            s = s + jnp.maximum(d, 0.0) * w[:, h:h + 1]
        key = jnp.where(k0 + lane_col <= row, _sortable_key(s), INT_MIN)
        key_scr[c] = key
        return 0

    lax.fori_loop(0, n_chunks, score_chunk, 0)

    def count(pred_fn):
        def body(c, acc):
            k0 = c * tk
            for j in range(tk // LANES):
                sl = slice(j * LANES, (j + 1) * LANES)
                acc = acc + jnp.where(pred_fn(key_scr[c, :, sl], k0 + j * LANES + lane_tile), 1.0, 0.0)
            return acc
        acc = lax.fori_loop(0, n_chunks, body, jnp.zeros((tq, LANES), F32))
        return jnp.broadcast_to(jnp.sum(acc, axis=1, keepdims=True), (tq, LANES))

    tau, jlim = _kth_threshold(count, topk, (tq, LANES), max(1, int(seq - 1).bit_length()))

    tau_c = jnp.broadcast_to(tau[:, 0:1], (tq, tk))
    jlim_c = jnp.broadcast_to(jlim[:, 0:1], (tq, tk))

    aq = q_ref[:, 0:512]
    per_group = A_HEADS // A_KV_HEADS
    qg = [_stack_heads(aq, g, per_group) for g in range(A_KV_HEADS)]

    def attn_chunk(c, carry):
        k0 = pl.multiple_of(c * tk, tk)
        key = key_scr[c]
        col = k0 + lane_col
        sel = ((key > tau_c) | ((key == tau_c) & (col <= jlim_c))) & (col <= row)
        kv = akv_ref[pl.ds(k0, tk), :]
        out = []
        for g in range(A_KV_HEADS):
            k = kv[:, g * HEAD_DIM:(g + 1) * HEAD_DIM].astype(BF16)
            v = kv[:, LANES + g * HEAD_DIM:LANES + (g + 1) * HEAD_DIM].astype(BF16)
            logits = lax.dot_general(qg[g], k, _NT, preferred_element_type=F32)
            logits = jnp.where(sel[None], logits.reshape(per_group, tq, tk), NEG).reshape(per_group * tq, tk)
            out.append(_online_softmax_step(carry[g], logits, v))
        return tuple(out)

    init = tuple(_softmax_init(per_group * tq, HEAD_DIM) for _ in range(A_KV_HEADS))
    res = lax.fori_loop(0, n_chunks, attn_chunk, init)
    heads = []
    for g in range(A_KV_HEADS):
        _, l, acc = res[g]
        o = acc / l[:, 0:HEAD_DIM]
        heads += [o[h * tq:(h + 1) * tq] for h in range(per_group)]
    o_ref[...] = jnp.concatenate(heads, axis=1).astype(BF16)


def dsa_prompt(q_all, misc, ik, akv, batch, seq, tq, tk):
    n = batch * seq
    nq = seq // tq
    kern = functools.partial(_dsa_prompt_kernel, tq=tq, tk=tk, topk=min(DSA_TOPK, seq // 4), seq=seq)
    qrow = lambda b, i: (b * nq + i, 0)
    return pl.pallas_call(
        kern,
        grid=(batch, nq),
        in_specs=[pl.BlockSpec((tq, AB_Q_W), qrow), pl.BlockSpec((tq, LANES), qrow),
                  pl.BlockSpec((seq, IDX_DIM), lambda b, i: (b, 0)),
                  pl.BlockSpec((seq, 256), lambda b, i: (b, 0))],
        out_specs=pl.BlockSpec((tq, A_HEADS * HEAD_DIM), qrow),
        out_shape=jax.ShapeDtypeStruct((n, A_HEADS * HEAD_DIM), BF16),
        scratch_shapes=[pltpu.VMEM((seq // tk, tq, tk), I32)],
        compiler_params=_cparams(("parallel", "parallel")),
        name="dsa_prompt",
    )(q_all, misc, ik, akv)


def _nsa_prompt_kernel(q_ref, misc_ref, ckv_ref, skv_ref, wkv_ref, o_ref, kc_scr, *, tq, tk, seq, wlen):
    i = pl.program_id(1)
    q0 = i * tq
    nc = seq // CMP_BLOCK
    n_top = min(SEL_TOPN, -(-seq // SEL_BLOCK))
    per_group = B_HEADS // B_KV_HEADS
    n_chunks = (q0 + tq + tk - 1) // tk

    @pl.when(i == 0)
    def _():
        rows = 8 * CMP_BLOCK
        def body(j, _):
            r0 = pl.multiple_of(j * rows, rows)
            x = ckv_ref[pl.ds(r0, rows), :].reshape(8, CMP_BLOCK, 256)
            kc_scr[pl.ds(pl.multiple_of(j * 8, 8), 8), :] = jnp.sum(x, axis=1) * (1.0 / CMP_BLOCK)
            return 0
        lax.fori_loop(0, seq // rows, body, 0)

    bq = q_ref[:, 1024:1536]
    gates = misc_ref[:, MISC_GATE:MISC_GATE + 3 * B_HEADS]
    kc = kc_scr[...]

    blk_t = lax.broadcasted_iota(I32, (nc, tq), 0)
    qpos_t = q0 + lax.broadcasted_iota(I32, (nc, tq), 1)
    visible = (blk_t + 1) * CMP_BLOCK - 1 <= qpos_t
    cur_t = qpos_t // SEL_BLOCK
    o_c, sel_t = [], []
    for g in range(B_KV_HEADS):
        kck = kc[:, g * HEAD_DIM:(g + 1) * HEAD_DIM].astype(BF16)
        kcv = kc[:, LANES + g * HEAD_DIM:LANES + (g + 1) * HEAD_DIM].astype(BF16)
        imp = jnp.zeros((nc, tq), F32)
        for h in range(per_group):
            hh = g * per_group + h
            qh = bq[:, hh * HEAD_DIM:(hh + 1) * HEAD_DIM]
            lc = lax.dot_general(kck, qh, _NT, preferred_element_type=F32)
            lc = jnp.where(visible, lc, -jnp.inf)
            m = jnp.max(lc, axis=0, keepdims=True)
            e = jnp.exp(lc - jnp.where(m > -jnp.inf, m, 0.0))
            ssum = jnp.sum(e, axis=0, keepdims=True)
            pc = e / jnp.where(ssum > 0, ssum, 1.0)
            imp = imp + pc
            o_c.append(lax.dot_general(pc.astype(BF16), kcv, _TN, preferred_element_type=F32))
        imp = jnp.where((blk_t == 0) | (blk_t == cur_t), jnp.inf, imp)
        imp = jnp.where(blk_t > cur_t, -jnp.inf, imp)
        rank = jnp.zeros((nc, tq), F32)
        for j in range(nc):
            r = imp[j:j + 1, :]
            rank = rank + jnp.where((r > imp) | ((r == imp) & (blk_t > j)), 1.0, 0.0)
        sel_t.append(jnp.where(rank < n_top, 1.0, 0.0).astype(BF16))

    qg = [_stack_heads(bq, g, per_group) for g in range(B_KV_HEADS)]
    row = q0 + lax.broadcasted_iota(I32, (tq, tk), 0)
    lane_col = lax.broadcasted_iota(I32, (tq, tk), 1)
    blk_e = lax.broadcasted_iota(I32, (nc, tk), 0)
    col_e = lax.broadcasted_iota(I32, (nc, tk), 1)

    def sel_chunk(c, carry):
        k0 = pl.multiple_of(c * tk, tk)
        expand = jnp.where((k0 + col_e) // SEL_BLOCK == blk_e, 1.0, 0.0).astype(BF16)
        causal = k0 + lane_col <= row
        kv = skv_ref[pl.ds(k0, tk), :]
        out = []
        for g in range(B_KV_HEADS):
            picked = lax.dot_general(sel_t[g], expand, _TN, preferred_element_type=F32) > 0.5
            k = kv[:, g * HEAD_DIM:(g + 1) * HEAD_DIM].astype(BF16)
            v = kv[:, LANES + g * HEAD_DIM:LANES + (g + 1) * HEAD_DIM].astype(BF16)
            logits = lax.dot_general(qg[g], k, _NT, preferred_element_type=F32)
            logits = jnp.where((picked & causal)[None], logits.reshape(per_group, tq, tk), NEG)
            out.append(_online_softmax_step(carry[g], logits.reshape(per_group * tq, tk), v))
        return tuple(out)

    init = tuple(_softmax_init(per_group * tq, HEAD_DIM) for _ in range(B_KV_HEADS))
    res = lax.fori_loop(0, n_chunks, sel_chunk, init)
    o_s = []
    for g in range(B_KV_HEADS):
        _, l, acc = res[g]
        o = acc / l[:, 0:HEAD_DIM]
        o_s += [o[h * tq:(h + 1) * tq] for h in range(per_group)]

    ws = pl.multiple_of(jnp.maximum(q0 + tq - wlen, 0), tq)
    wkv = wkv_ref[pl.ds(ws, wlen), :]
    dist = (q0 + lax.broadcasted_iota(I32, (tq, wlen), 0)) - (ws + lax.broadcasted_iota(I32, (tq, wlen), 1))
    in_win = (dist >= 0) & (dist <= WINDOW)
    o_w = []
    for g in range(B_KV_HEADS):
        k = wkv[:, g * HEAD_DIM:(g + 1) * HEAD_DIM].astype(BF16)
        v = wkv[:, LANES + g * HEAD_DIM:LANES + (g + 1) * HEAD_DIM].astype(BF16)
        logits = lax.dot_general(qg[g], k, _NT, preferred_element_type=F32)
        logits = jnp.where(in_win[None], logits.reshape(per_group, tq, wlen), NEG).reshape(per_group * tq, wlen)
        m = jnp.max(logits, axis=1, keepdims=True)
        p = jnp.exp(logits - m)
        o = jnp.dot(p.astype(BF16), v, preferred_element_type=F32) / jnp.sum(p, axis=1, keepdims=True)
        o_w += [o[h * tq:(h + 1) * tq] for h in range(per_group)]

    heads = []
    for h in range(B_HEADS):
        gate = lambda br: gates[:, br * B_HEADS + h: br * B_HEADS + h + 1]
        heads.append(gate(0) * o_c[h] + gate(1) * o_s[h] + gate(2) * o_w[h])
    o_ref[...] = jnp.concatenate(heads, axis=1).astype(BF16)


def nsa_prompt(q_all, misc, ckv, skv, wkv, batch, seq, tq, tk):
    n = batch * seq
    nq = seq // tq
    wlen = min(WINDOW + tq, seq)
    kern = functools.partial(_nsa_prompt_kernel, tq=tq, tk=tk, seq=seq, wlen=wlen)
    qrow = lambda b, i: (b * nq + i, 0)
    full = lambda b, i: (b, 0)
    return pl.pallas_call(
        kern,
        grid=(batch, nq),
        in_specs=[pl.BlockSpec((tq, AB_Q_W), qrow), pl.BlockSpec((tq, LANES), qrow),
                  pl.BlockSpec((seq, 256), full), pl.BlockSpec((seq, 256), full), pl.BlockSpec((seq, 256), full)],
        out_specs=pl.BlockSpec((tq, B_HEADS * HEAD_DIM), qrow),
        out_shape=jax.ShapeDtypeStruct((n, B_HEADS * HEAD_DIM), BF16),
        scratch_shapes=[pltpu.VMEM((seq // CMP_BLOCK, 256), F32)],
        compiler_params=_cparams(("parallel", "arbitrary")),
        name="nsa_prompt",
    )(q_all, misc, ckv, skv, wkv)


MLA_CAT = C_HEADS * LANES
MLA_DOWN_COLS = C_Q_RANK + C_KV_RANK + LANES
ROPE_LO = C_NOPE


def _prep_mla(w_down, w_uq, w_uk, w_uv):
    cq, ckv, kpe = jnp.split(w_down, [C_Q_RANK, C_Q_RANK + C_KV_RANK], axis=1)
    z = lambda n: jnp.zeros((w_down.shape[0], n), w_down.dtype)
    w_down_r = jnp.concatenate([cq, ckv, z(ROPE_LO), kpe, z(LANES - ROPE_LO - C_ROPE)], axis=1).astype(BF16)
    uq = w_uq.reshape(C_Q_RANK, C_HEADS, C_NOPE + C_ROPE)
    uq = jnp.pad(uq, ((0, 0), (0, 0), (0, LANES - C_NOPE - C_ROPE))).reshape(C_Q_RANK, MLA_CAT).astype(BF16)
    uk = jnp.pad(w_uk, ((0, 0), (0, 0), (0, LANES - C_NOPE))).reshape(C_KV_RANK, MLA_CAT).astype(BF16)
    uv = w_uv.reshape(C_KV_RANK, C_HEADS * C_VDIM).astype(BF16)
    return w_down_r, uq, uk, uv


def _rms(x, g):
    return x * lax.rsqrt(jnp.mean(x * x, axis=-1, keepdims=True) + RMS_EPS) * g


def _mla_proj_kernel(x_ref, wd_ref, uq_ref, uk_ref, uv_ref, gq_ref, gkv_ref, c_ref, s_ref,
                     qcat_ref, kcat_ref, v_ref, ckv_ref, kpe_ref):
    xb = x_ref[...].astype(BF16)
    tab_c, tab_s = c_ref[...], s_ref[...]
    lane = lax.broadcasted_iota(I32, tab_c.shape, 1)
    first = (lane % C_ROPE) < C_ROPE // 2
    rope = lambda v: _swap_rope(v, tab_c, tab_s, first, C_ROPE // 2)

    c = jnp.dot(xb, wd_ref[...], preferred_element_type=F32)
    cq = _rms(c[:, 0:C_Q_RANK], gq_ref[...]).astype(BF16)
    ckv = _rms(c[:, C_Q_RANK:C_Q_RANK + C_KV_RANK], gkv_ref[...])
    kpe = rope(c[:, C_Q_RANK + C_KV_RANK:])
    ckv_ref[...] = ckv
    kpe_ref[...] = kpe[:, ROPE_LO:ROPE_LO + C_ROPE]
    ckv_b = ckv.astype(BF16)
    v_ref[...] = jnp.dot(ckv_b, uv_ref[...], preferred_element_type=F32).astype(BF16)
    for h in range(C_HEADS):
        sl = slice(h * LANES, (h + 1) * LANES)
        q = jnp.dot(cq, uq_ref[:, sl], preferred_element_type=F32)
        qcat_ref[:, sl] = rope(q).astype(BF16)
        k = jnp.dot(ckv_b, uk_ref[:, sl], preferred_element_type=F32)
        kcat_ref[:, sl] = (k + kpe).astype(BF16)


def mla_proj(x, w_down_r, uq, uk, uv, g_q, g_kv, tab_c, tab_s, tm):
    n, d = x.shape
    n_tab = tab_c.shape[0] // tm
    row = lambda i: (i, 0)
    tab = lambda i: (i % n_tab, 0)
    out_shape = (
        jax.ShapeDtypeStruct((n, MLA_CAT), BF16), jax.ShapeDtypeStruct((n, MLA_CAT), BF16),
        jax.ShapeDtypeStruct((n, C_HEADS * C_VDIM), BF16),
        jax.ShapeDtypeStruct((n, C_KV_RANK), F32), jax.ShapeDtypeStruct((n, C_ROPE), F32),
    )
    out_specs = (
        pl.BlockSpec((tm, MLA_CAT), row), pl.BlockSpec((tm, MLA_CAT), row),
        pl.BlockSpec((tm, C_HEADS * C_VDIM), row),
        pl.BlockSpec((tm, C_KV_RANK), row), pl.BlockSpec((tm, C_ROPE), row),
    )
    return pl.pallas_call(
        _mla_proj_kernel,
        grid=(n // tm,),
        in_specs=[pl.BlockSpec((tm, d), row), _resident(w_down_r.shape), _resident(uq.shape),
                  _resident(uk.shape), _resident(uv.shape),
                  _resident((1, C_Q_RANK)), _resident((1, C_KV_RANK)),
                  pl.BlockSpec((tm, LANES), tab), pl.BlockSpec((tm, LANES), tab)],
        out_specs=out_specs,
        out_shape=out_shape,
        compiler_params=_cparams(("parallel",)),
        name="mla_proj",
    )(x, w_down_r, uq, uk, uv, g_q.reshape(1, -1), g_kv.reshape(1, -1), tab_c, tab_s)


def _mla_attn_kernel(q_ref, k_ref, v_ref, o_ref, m_scr, l_scr, acc_scr, *, tq, tk):
    i, j = pl.program_id(1), pl.program_id(2)
    scale = (C_NOPE + C_ROPE) ** -0.5

    @pl.when(j == 0)
    def _():
        m_scr[...] = jnp.full(m_scr.shape, NEG, F32)
        l_scr[...] = jnp.zeros(l_scr.shape, F32)
        acc_scr[...] = jnp.zeros(acc_scr.shape, F32)

    def update(on_diagonal):
        if on_diagonal:
            row = i * tq + lax.broadcasted_iota(I32, (tq, tk), 0)
            col = j * tk + lax.broadcasted_iota(I32, (tq, tk), 1)
            causal = col <= row
        for h in range(C_HEADS):
            q = q_ref[:, h * LANES:(h + 1) * LANES]
            k = k_ref[:, h * LANES:(h + 1) * LANES]
            v = v_ref[:, h * C_VDIM:(h + 1) * C_VDIM]
            logits = lax.dot_general(q, k, _NT, preferred_element_type=F32) * scale
            if on_diagonal:
                logits = jnp.where(causal, logits, NEG)
            m_scr[h], l_scr[h], acc_scr[h] = _online_softmax_step((m_scr[h], l_scr[h], acc_scr[h]), logits, v)

    needed = j * tk <= i * tq + tq - 1
    crosses = j * tk + tk - 1 > i * tq
    pl.when(needed & crosses)(lambda: update(True))
    pl.when(needed & jnp.logical_not(crosses))(lambda: update(False))

    @pl.when(j == pl.num_programs(2) - 1)
    def _():
        o_ref[...] = jnp.concatenate(
            [acc_scr[h] / l_scr[h][:, 0:C_VDIM] for h in range(C_HEADS)], axis=1).astype(BF16)


def mla_attn(qcat, kcat, v, batch, seq, tq, tk):
    n = batch * seq
    nq, nk = seq // tq, seq // tk
    kern = functools.partial(_mla_attn_kernel, tq=tq, tk=tk)
    last = lambda i: (i * tq + tq - 1) // tk
    kv_row = lambda b, i, j: (b * nk + jnp.minimum(j, last(i)), 0)
    return pl.pallas_call(
        kern,
        grid=(batch, nq, nk),
        in_specs=[pl.BlockSpec((tq, MLA_CAT), lambda b, i, j: (b * nq + i, 0)),
                  pl.BlockSpec((tk, MLA_CAT), kv_row),
                  pl.BlockSpec((tk, C_HEADS * C_VDIM), kv_row)],
        out_specs=pl.BlockSpec((tq, C_HEADS * C_VDIM), lambda b, i, j: (b * nq + i, 0)),
        out_shape=jax.ShapeDtypeStruct((n, C_HEADS * C_VDIM), BF16),
        scratch_shapes=[pltpu.VMEM((C_HEADS, tq, LANES), F32), pltpu.VMEM((C_HEADS, tq, LANES), F32),
                        pltpu.VMEM((C_HEADS, tq, C_VDIM), F32)],
        compiler_params=_cparams(("parallel", "parallel", "arbitrary")),
        name="mla_attn",
    )(qcat, kcat, v)


PAGES_PER_CHUNK = 16
Q_ROWS = 8


def _chunk_copies(pt_ref, pool_ref, buf_ref, sem, b, c, slot, ppc):
    return [pltpu.make_async_copy(pool_ref.at[pt_ref[b, c * ppc + p]],
                                  buf_ref.at[slot, pl.ds(p * PAGE_SIZE, PAGE_SIZE)], sem.at[slot])
            for p in range(ppc)]


def _stream_chunk(pt_ref, pools, bufs, sems, ppc):
    b, c = pl.program_id(0), pl.program_id(1)
    n_c = pl.num_programs(1)
    step = b * n_c + c
    slot = step % 2

    def start(bb, cc, s):
        for pool, buf, sem in zip(pools, bufs, sems):
            for cp in _chunk_copies(pt_ref, pool, buf, sem, bb, cc, s, ppc):
                cp.start()

    @pl.when(step == 0)
    def _():
        start(b, c, slot)

    @pl.when(step + 1 < pl.num_programs(0) * n_c)
    def _():
        wrap = c + 1 == n_c
        start(jnp.where(wrap, b + 1, b), jnp.where(wrap, 0, c + 1), 1 - slot)

    for pool, buf, sem in zip(pools, bufs, sems):
        for cp in _chunk_copies(pt_ref, pool, buf, sem, b, c, slot, ppc):
            cp.wait()
    return slot


def _paged_call(kern, name, page_table, n_chunks, in_arrays, in_specs, pools, out_shape, out_specs, scratch):
    bs = page_table.shape[0]
    any_spec = pl.BlockSpec(memory_space=pl.ANY)
    return pl.pallas_call(
        kern,
        grid_spec=pltpu.PrefetchScalarGridSpec(
            num_scalar_prefetch=1, grid=(bs, n_chunks),
            in_specs=list(in_specs) + [any_spec] * len(pools),
            out_specs=out_specs, scratch_shapes=scratch),
        out_shape=out_shape,
        compiler_params=_cparams(("arbitrary", "arbitrary")),
        name=name,
    )(page_table, *in_arrays, *pools)


def _per_seq(shape):
    nd = len(shape)
    return pl.BlockSpec((1,) + tuple(shape), lambda b, c, pt: (b,) + (0,) * nd)


def _group_rows(x0, x1):
    r = lax.broadcasted_iota(I32, x0.shape, 0)
    return jnp.where(r < Q_ROWS // 2, x0, x1)


def _new_row_terms(q, new_row):
    nb = new_row.astype(BF16).astype(F32)
    bc = lambda a: jnp.broadcast_to(a, (Q_ROWS, HEAD_DIM))
    k = _group_rows(bc(nb[:, 0:HEAD_DIM]), bc(nb[:, HEAD_DIM:2 * HEAD_DIM]))
    v = _group_rows(bc(nb[:, LANES:LANES + HEAD_DIM]), bc(nb[:, LANES + HEAD_DIM:2 * LANES]))
    return jnp.sum(q.astype(F32) * k, axis=1, keepdims=True), v


def _gqa_decode_chunk(q, kv, mask, carry):
    kvb = kv.astype(BF16)
    lg = [lax.dot_general(q, kvb[:, g * HEAD_DIM:(g + 1) * HEAD_DIM], _NT, preferred_element_type=F32)
          for g in range(2)]
    logits = jnp.where(mask, _group_rows(lg[0], lg[1]), NEG)
    m, l, acc = carry
    m_new = jnp.maximum(m, jnp.max(logits, axis=1, keepdims=True))
    alpha = jnp.exp(m - m_new)
    p = jnp.exp(logits - m_new)
    pb = p.astype(BF16)
    pv = [jnp.dot(pb, kvb[:, LANES + g * HEAD_DIM:LANES + (g + 1) * HEAD_DIM], preferred_element_type=F32)
          for g in range(2)]
    return m_new, alpha * l + jnp.sum(p, axis=1, keepdims=True), alpha * acc + _group_rows(pv[0], pv[1])


def _dsa_sample_index_kernel(pt_ref, iq_ref, w_ref, ikn_ref, pool_ref, keys_ref, meta_ref, buf, sem,
                             *, ppc, topk, past):
    c = pl.program_id(1)
    ck = ppc * PAGE_SIZE
    slot = _stream_chunk(pt_ref, [pool_ref], [buf], [sem], ppc)
    iq = iq_ref[0]
    w = w_ref[0]
    rows = keys_ref.shape[1]

    @pl.when(c == 0)
    def _():
        keys_ref[0] = jnp.full((rows, ck), INT_MIN, I32)

    d = lax.dot_general(iq, buf[slot].astype(BF16), _NT, preferred_element_type=F32)
    s = jnp.sum(jnp.maximum(d, 0.0) * w, axis=0, keepdims=True)
    keys_ref[0, pl.ds(c, 1), :] = _sortable_key(s)

    @pl.when(c == pl.num_programs(1) - 1)
    def _():
        dn = jnp.sum(iq.astype(F32) * ikn_ref[0].astype(BF16).astype(F32), axis=1, keepdims=True)
        sn = jnp.sum(jnp.maximum(dn, 0.0) * w, axis=0, keepdims=True)
        key_new = _sortable_key(jnp.broadcast_to(sn, (rows, LANES)))
        idx_new = jnp.full((rows, LANES), past, I32)
        idx_tile = (lax.broadcasted_iota(I32, (rows, LANES), 0) * ck + lax.broadcasted_iota(I32, (rows, LANES), 1))

        def count(pred):
            acc = jnp.zeros((rows, LANES), F32)
            for j in range(ck // LANES):
                acc = acc + jnp.where(pred(keys_ref[0, :, j * LANES:(j + 1) * LANES], idx_tile + j * LANES), 1.0, 0.0)
            tot = jnp.sum(jnp.sum(acc, axis=1, keepdims=True), axis=0, keepdims=True)
            return jnp.broadcast_to(tot, (rows, LANES)) + jnp.where(pred(key_new, idx_new), 1.0, 0.0)

        tau, jlim = _kth_threshold(count, topk, (rows, LANES), int(past).bit_length())
        meta_ref[0, 0] = tau
        meta_ref[0, 1] = jlim
        meta_ref[0, 2] = key_new


def dsa_sample_index(page_table, iq, w, ik_new, idx_pool, ppc):
    bs, n_pages = page_table.shape
    n_c = n_pages // ppc
    ck = ppc * PAGE_SIZE
    past = n_pages * PAGE_SIZE
    rows = -(-n_c // 8) * 8
    kern = functools.partial(_dsa_sample_index_kernel, ppc=ppc, topk=min(DSA_TOPK, (past + 1) // 4), past=past)
    return _paged_call(
        kern, "dsa_sample_index", page_table, n_c, [iq, w, ik_new],
        [_per_seq((Q_ROWS, IDX_DIM)), _per_seq((Q_ROWS, 1)), _per_seq((1, IDX_DIM))], [idx_pool],
        (jax.ShapeDtypeStruct((bs, rows, ck), I32), jax.ShapeDtypeStruct((bs, 3, rows, LANES), I32)),
        (_per_seq((rows, ck)), _per_seq((3, rows, LANES))),
        [pltpu.VMEM((2, ck, IDX_DIM), F32), pltpu.SemaphoreType.DMA((2,))])


def _dsa_sample_attn_kernel(pt_ref, q_ref, new_ref, keys_ref, meta_ref, pool_ref, o_ref, buf, sem,
                            m_scr, l_scr, acc_scr, *, ppc, past):
    c = pl.program_id(1)
    ck = ppc * PAGE_SIZE
    slot = _stream_chunk(pt_ref, [pool_ref], [buf], [sem], ppc)
    q = q_ref[0]
    tau, jlim, key_new = meta_ref[0, 0], meta_ref[0, 1], meta_ref[0, 2]

    @pl.when(c == 0)
    def _():
        logit, v = _new_row_terms(q, new_ref[0])
        t8, j8, k8 = tau[0:Q_ROWS, 0:1], jlim[0:Q_ROWS, 0:1], key_new[0:Q_ROWS, 0:1]
        picked = (k8 > t8) | ((k8 == t8) & (j8 >= past))
        m_scr[...] = jnp.where(picked, logit, NEG)
        l_scr[...] = jnp.where(picked, 1.0, 0.0)
        acc_scr[...] = jnp.where(picked, v, 0.0)

    key = keys_ref[0, pl.ds(c, 1), :]
    idx = c * ck + lax.broadcasted_iota(I32, (1, ck), 1)
    t1, j1 = tau[0:1, 0:1], jlim[0:1, 0:1]
    sel = (key > t1) | ((key == t1) & (idx <= j1))
    m, l, acc = _gqa_decode_chunk(q, buf[slot], sel, (m_scr[...], l_scr[...], acc_scr[...]))
    m_scr[...], l_scr[...], acc_scr[...] = m, l, acc

    @pl.when(c == pl.num_programs(1) - 1)
    def _():
        o_ref[0] = (acc / l).astype(BF16)


def dsa_sample_attn(page_table, q, new_row, keys, meta, kv_pool, ppc):
    bs, n_pages = page_table.shape
    ck = ppc * PAGE_SIZE
    rows = keys.shape[1]
    kern = functools.partial(_dsa_sample_attn_kernel, ppc=ppc, past=n_pages * PAGE_SIZE)
    return _paged_call(
        kern, "dsa_sample_attn", page_table, n_pages // ppc, [q, new_row, keys, meta],
        [_per_seq((Q_ROWS, HEAD_DIM)), _per_seq((1, 256)), _per_seq((rows, ck)), _per_seq((3, rows, LANES))],
        [kv_pool],
        jax.ShapeDtypeStruct((bs, Q_ROWS, HEAD_DIM), BF16), _per_seq((Q_ROWS, HEAD_DIM)),
        [pltpu.VMEM((2, ck, 256), F32), pltpu.SemaphoreType.DMA((2,)),
         pltpu.VMEM((Q_ROWS, 1), F32), pltpu.VMEM((Q_ROWS, 1), F32), pltpu.VMEM((Q_ROWS, HEAD_DIM), F32)])


def _nsa_sample_cmp_kernel(pt_ref, q_ref, pool_ref, oc_ref, sel_ref, buf, sem, kc_scr, *, ppc, past, n_keep):
    c = pl.program_id(1)
    ck = ppc * PAGE_SIZE
    bpc = ck // CMP_BLOCK
    slot = _stream_chunk(pt_ref, [pool_ref], [buf], [sem], ppc)
    x = buf[slot].reshape(bpc, CMP_BLOCK, 256)
    kc_scr[pl.ds(pl.multiple_of(c * bpc, bpc), bpc), :] = jnp.sum(x, axis=1) * (1.0 / CMP_BLOCK)

    @pl.when(c == pl.num_programs(1) - 1)
    def _():
        q = q_ref[0]
        kcb = kc_scr[...].astype(BF16)
        nc = kcb.shape[0]
        lc = _group_rows(*[lax.dot_general(q, kcb[:, g * HEAD_DIM:(g + 1) * HEAD_DIM], _NT,
                                           preferred_element_type=F32) for g in range(2)])
        blk = lax.broadcasted_iota(I32, (Q_ROWS, nc), 1)
        row = lax.broadcasted_iota(I32, (Q_ROWS, nc), 0)
        lc = jnp.where((blk + 1) * CMP_BLOCK - 1 <= past, lc, -jnp.inf)
        m = jnp.max(lc, axis=1, keepdims=True)
        e = jnp.exp(lc - jnp.where(m > -jnp.inf, m, 0.0))
        ssum = jnp.sum(e, axis=1, keepdims=True)
        pc = e / jnp.where(ssum > 0, ssum, 1.0)
        pcb = pc.astype(BF16)
        oc_ref[0] = _group_rows(*[jnp.dot(pcb, kcb[:, LANES + g * HEAD_DIM:LANES + (g + 1) * HEAD_DIM],
                                          preferred_element_type=F32) for g in range(2)])
        imp0 = jnp.sum(jnp.where(row < Q_ROWS // 2, pc, 0.0), axis=0, keepdims=True)
        imp1 = jnp.sum(jnp.where(row >= Q_ROWS // 2, pc, 0.0), axis=0, keepdims=True)
        imp = jnp.where(row == 0, imp0, jnp.where(row == 1, imp1, -jnp.inf))
        imp = jnp.where((blk == 0) & (row < 2), jnp.inf, imp)
        key = _sortable_key(imp)
        blk_tile = lax.broadcasted_iota(I32, (Q_ROWS, LANES), 1)

        def count(pred):
            acc = jnp.zeros((Q_ROWS, LANES), F32)
            for j in range(nc // LANES):
                acc = acc + jnp.where(pred(key[:, j * LANES:(j + 1) * LANES], blk_tile + j * LANES), 1.0, 0.0)
            return jnp.broadcast_to(jnp.sum(acc, axis=1, keepdims=True), (Q_ROWS, LANES))

        tau, jlim = _kth_threshold(count, n_keep, (Q_ROWS, LANES), max(1, int(nc - 1).bit_length()))
        for j in range(nc // LANES):
            kj = key[:, j * LANES:(j + 1) * LANES]
            sel_ref[0, :, j * LANES:(j + 1) * LANES] = jnp.where(
                (kj > tau) | ((kj == tau) & (blk_tile + j * LANES <= jlim)), 1.0, 0.0)


def nsa_sample_cmp(page_table, q, cmp_pool, ppc):
    bs, n_pages = page_table.shape
    past = n_pages * PAGE_SIZE
    nc = past // CMP_BLOCK
    ck = ppc * PAGE_SIZE
    n_blk = -(-(past + 1) // SEL_BLOCK)
    kern = functools.partial(_nsa_sample_cmp_kernel, ppc=ppc, past=past, n_keep=min(SEL_TOPN, n_blk) - 1)
    return _paged_call(
        kern, "nsa_sample_cmp", page_table, n_pages // ppc, [q], [_per_seq((Q_ROWS, HEAD_DIM))], [cmp_pool],
        (jax.ShapeDtypeStruct((bs, Q_ROWS, HEAD_DIM), F32), jax.ShapeDtypeStruct((bs, Q_ROWS, nc), F32)),
        (_per_seq((Q_ROWS, HEAD_DIM)), _per_seq((Q_ROWS, nc))),
        [pltpu.VMEM((2, ck, 256), F32), pltpu.SemaphoreType.DMA((2,)), pltpu.VMEM((nc, 256), F32)])


def _nsa_sample_attn_kernel(pt_ref, q_ref, new_ref, blk_ref, exp_ref, oc_ref, gate_ref, win_ref, wnew_ref,
                            pool_ref, o_ref, buf, sem, m_scr, l_scr, acc_scr, *, ppc):
    c = pl.program_id(1)
    slot = _stream_chunk(pt_ref, [pool_ref], [buf], [sem], ppc)
    q = q_ref[0]

    @pl.when(c == 0)
    def _():
        logit, v = _new_row_terms(q, new_ref[0])
        m_scr[...] = logit
        l_scr[...] = jnp.ones((Q_ROWS, 1), F32)
        acc_scr[...] = v

    picked = jnp.dot(blk_ref[0, 0].astype(BF16), exp_ref[...], preferred_element_type=F32) > 0.5
    m, l, acc = _gqa_decode_chunk(q, buf[slot], picked, (m_scr[...], l_scr[...], acc_scr[...]))
    m_scr[...], l_scr[...], acc_scr[...] = m, l, acc

    @pl.when(c == pl.num_programs(1) - 1)
    def _():
        o_s = acc / l
        logit, v = _new_row_terms(q, wnew_ref[0])
        win = win_ref[0]
        every = jnp.full((1, win.shape[0]), True)
        _, lw, accw = _gqa_decode_chunk(q, win, every, (logit, jnp.ones((Q_ROWS, 1), F32), v))
        g = gate_ref[0]
        o_ref[0] = (g[:, 0:1] * oc_ref[0] + g[:, 1:2] * o_s + g[:, 2:3] * (accw / lw)).astype(BF16)


def nsa_sample_attn(page_table, q, new_row, blk, expand, o_c, gates, win_state, win_new, sel_pool, ppc):
    bs, n_pages = page_table.shape
    ck = ppc * PAGE_SIZE
    bpc = ck // SEL_BLOCK
    nbuf = win_state.shape[1]
    kern = functools.partial(_nsa_sample_attn_kernel, ppc=ppc)
    return _paged_call(
        kern, "nsa_sample_attn", page_table, n_pages // ppc,
        [q, new_row, blk, expand, o_c, gates, win_state, win_new],
        [_per_seq((Q_ROWS, HEAD_DIM)), _per_seq((1, 256)),
         pl.BlockSpec((1, 1, Q_ROWS, bpc), lambda b, c, pt: (b, c, 0, 0)),
         pl.BlockSpec((bpc, ck), lambda b, c, pt: (0, 0)),
         _per_seq((Q_ROWS, HEAD_DIM)), _per_seq((Q_ROWS, 3)), _per_seq((nbuf, 256)), _per_seq((1, 256))],
        [sel_pool],
        jax.ShapeDtypeStruct((bs, Q_ROWS, HEAD_DIM), BF16), _per_seq((Q_ROWS, HEAD_DIM)),
        [pltpu.VMEM((2, ck, 256), F32), pltpu.SemaphoreType.DMA((2,)),
         pltpu.VMEM((Q_ROWS, 1), F32), pltpu.VMEM((Q_ROWS, 1), F32), pltpu.VMEM((Q_ROWS, HEAD_DIM), F32)])


def _mla_sample_attn_kernel(pt_ref, ql_ref, qp_ref, cn_ref, pn_ref, ckv_pool, kpe_pool, o_ref,
                            cbuf, pbuf, csem, psem, m_scr, l_scr, acc_scr, *, ppc):
    c = pl.program_id(1)
    slot = _stream_chunk(pt_ref, [ckv_pool, kpe_pool], [cbuf, pbuf], [csem, psem], ppc)
    scale = (C_NOPE + C_ROPE) ** -0.5
    ql, qp = ql_ref[0], qp_ref[0]

    @pl.when(c == 0)
    def _():
        cn = cn_ref[0].astype(BF16).astype(F32)
        pn = pn_ref[0].astype(BF16).astype(F32)
        logit = (jnp.sum(ql.astype(F32) * cn, axis=1, keepdims=True)
                 + jnp.sum(qp.astype(F32) * pn, axis=1, keepdims=True)) * scale
        m_scr[...] = jnp.broadcast_to(logit, (C_HEADS, LANES))
        l_scr[...] = jnp.ones((C_HEADS, LANES), F32)
        acc_scr[...] = jnp.broadcast_to(cn, (C_HEADS, C_KV_RANK))

    cb = cbuf[slot].astype(BF16)
    pb = pbuf[slot].astype(BF16)
    logits = (lax.dot_general(ql, cb, _NT, preferred_element_type=F32)
              + lax.dot_general(qp, pb, _NT, preferred_element_type=F32)) * scale
    m, l, acc = _online_softmax_step((m_scr[...], l_scr[...], acc_scr[...]), logits, cb)
    m_scr[...], l_scr[...], acc_scr[...] = m, l, acc

    @pl.when(c == pl.num_programs(1) - 1)
    def _():
        o_ref[0] = acc / _lanes_to(l, C_KV_RANK)


def mla_sample_attn(page_table, q_lat, q_pe, ckv_new, kpe_new, ckv_pool, kpe_pool, ppc):
    bs, n_pages = page_table.shape
    ck = ppc * PAGE_SIZE
    kern = functools.partial(_mla_sample_attn_kernel, ppc=ppc)
    return _paged_call(
        kern, "mla_sample_attn", page_table, n_pages // ppc, [q_lat, q_pe, ckv_new, kpe_new],
        [_per_seq((C_HEADS, C_KV_RANK)), _per_seq((C_HEADS, C_ROPE)), _per_seq((1, C_KV_RANK)),
         _per_seq((1, C_ROPE))],
        [ckv_pool, kpe_pool],
        jax.ShapeDtypeStruct((bs, C_HEADS, C_KV_RANK), F32), _per_seq((C_HEADS, C_KV_RANK)),
        [pltpu.VMEM((2, ck, C_KV_RANK), F32), pltpu.VMEM((2, ck, C_ROPE), F32),
         pltpu.SemaphoreType.DMA((2,)), pltpu.SemaphoreType.DMA((2,)),
         pltpu.VMEM((C_HEADS, LANES), F32), pltpu.VMEM((C_HEADS, LANES), F32),
         pltpu.VMEM((C_HEADS, C_KV_RANK), F32)])


def _head_bmm_kernel(x_ref, w_ref, o_ref):
    o_ref[0] = jnp.dot(x_ref[0].astype(BF16), w_ref[0].astype(BF16), preferred_element_type=F32).astype(o_ref.dtype)


def head_bmm(x, w):
    h, b, k = x.shape
    n = w.shape[2]
    return pl.pallas_call(
        _head_bmm_kernel,
        grid=(h,),
        in_specs=[pl.BlockSpec((1, b, k), lambda i: (i, 0, 0)), pl.BlockSpec((1, k, n), lambda i: (i, 0, 0))],
        out_specs=pl.BlockSpec((1, b, n), lambda i: (i, 0, 0)),
        out_shape=jax.ShapeDtypeStruct((h, b, n), BF16),
        compiler_params=_cparams(("parallel",)),
        name="head_bmm",
    )(x, w)


TM = 512
TQ_SPARSE = 256
TK_SPARSE = 512
TQ_MLA, TK_MLA = 512, 512


def kernel(x_prompt, x_sample, cache_dsa_kv, cache_dsa_idx, cache_nsa_cmp_kv, cache_nsa_sel_kv, state_nsa_win_kv,
           cache_mla_ckv, cache_mla_kpe, page_table, ln_g, ln_b, ffn_w_gate, ffn_w_up, ffn_w_down, ab_w_in,
           ab_w_out, mla_w_down, mla_g_q, mla_g_kv, mla_w_uq, mla_w_uk, mla_w_uv, mla_w_out):
    batch, seq, d = x_prompt.shape
    n = batch * seq
    tm = min(TM, n)
    tq_s, tk_s = min(TQ_SPARSE, seq), min(TK_SPARSE, seq)
    tq_m, tk_m = min(TQ_MLA, seq), min(TK_MLA, seq)
    pos = jnp.arange(seq)

    bs, dec_seq, _ = x_sample.shape
    n_pool = cache_dsa_kv.shape[0]
    n_pages = page_table.shape[1]
    past = n_pages * PAGE_SIZE
    nbuf = state_nsa_win_kv.shape[1]
    ppc = min(PAGES_PER_CHUNK, n_pages)
    n_chunks = n_pages // ppc
    ck = ppc * PAGE_SIZE
    assert dec_seq == 1 and n_pages % ppc == 0 and past % SEL_BLOCK == 0 and nbuf <= min(WINDOW, past)
    assert (past // CMP_BLOCK) % LANES == 0
    spos = jnp.full((bs,), past, I32)

    yp = x_prompt.reshape(n, d)
    ys = x_sample.reshape(bs, d)
    ffn_w = [[_prep_ffn(ffn_w_gate[l, j], ffn_w_up[l, j], ffn_w_down[l, j]) for j in range(2)] for l in range(DEPTH)]

    g, b = ln_g[0], ln_b[0]
    yp = ffn_ln(yp, *ffn_w[0][0], g[0], b[0], tm)
    ys = ffn_ln(ys, *ffn_w[0][0], g[0], b[0], bs)

    w_in_b = _prep_ab_w_in(ab_w_in)
    w_out_b = ab_w_out.astype(BF16)
    half = A_HEADS * HEAD_DIM
    tab_c, tab_s = _rope_tables(pos, HEAD_DIM, 0, LANES)
    q_all, akv, ckv, skv, wkv, ik, misc = ab_proj(yp, w_in_b, tab_c, tab_s, tm)
    o_a = dsa_prompt(q_all, misc, ik, akv, batch, seq, tq_s, tk_s)
    o_b = nsa_prompt(q_all, misc, ckv, skv, wkv, batch, seq, tq_s, tk_s)
    yp = proj_ln(yp, [o_a, o_b], [w_out_b[:half], w_out_b[half:]], g[1], b[1], tm)

    stab_c, stab_s = _rope_tables(spos, HEAD_DIM, 0, LANES)
    q_all_s, akv_s, ckv_s, skv_s, wkv_s, ik_s, misc_s = ab_proj(ys, w_in_b, stab_c, stab_s, bs)
    heads3 = lambda a: a.reshape(bs, Q_ROWS, HEAD_DIM)
    aq_s, iq_s, bq_s = heads3(q_all_s[:, 0:512]), heads3(q_all_s[:, 512:1024]), heads3(q_all_s[:, 1024:1536])
    iw_s = misc_s[:, MISC_IW:MISC_IW + IDX_HEADS].reshape(bs, IDX_HEADS, 1)
    gates_s = misc_s[:, MISC_GATE:MISC_GATE + 3 * B_HEADS].reshape(bs, 3, B_HEADS).transpose(0, 2, 1)
    row3 = lambda a: a.reshape(bs, 1, a.shape[-1])
    pool3 = lambda a: a.reshape(n_pool, PAGE_SIZE, -1)

    keys, meta = dsa_sample_index(page_table, iq_s, iw_s, row3(ik_s), cache_dsa_idx, ppc)
    o_a_s = dsa_sample_attn(page_table, aq_s, row3(akv_s), keys, meta, pool3(cache_dsa_kv), ppc)
    o_c_s, blk_sel = nsa_sample_cmp(page_table, bq_s, pool3(cache_nsa_cmp_kv), ppc)
    bpc = ck // SEL_BLOCK
    blk_heads = jnp.repeat(blk_sel[:, 0:B_KV_HEADS], B_HEADS // B_KV_HEADS, axis=1)
    blk_heads = blk_heads.reshape(bs, Q_ROWS, n_chunks, bpc).transpose(0, 2, 1, 3)
    expand = (jnp.arange(ck)[None, :] // SEL_BLOCK == jnp.arange(bpc)[:, None]).astype(BF16)
    o_b_s = nsa_sample_attn(page_table, bq_s, row3(skv_s), blk_heads, expand, o_c_s, gates_s,
                            state_nsa_win_kv.reshape(bs, nbuf, -1), row3(wkv_s), pool3(cache_nsa_sel_kv), ppc)
    ys = proj_ln(ys, [o_a_s.reshape(bs, half), o_b_s.reshape(bs, half)], [w_out_b[:half], w_out_b[half:]],
                 g[1], b[1], bs)

    yp = ffn_ln(yp, *ffn_w[0][1], g[2], b[2], tm)
    ys = ffn_ln(ys, *ffn_w[0][1], g[2], b[2], bs)

    kv5 = lambda a, t: a.reshape(-1, t, 2, 2, HEAD_DIM)
    dsa_kv_p, cmp_kv_p, sel_kv_p = kv5(akv, seq), kv5(ckv, seq), kv5(skv, seq)
    win_kv_p = kv5(wkv, seq)[:, -min(WINDOW, seq):]
    dsa_idx_p = ik.reshape(batch, seq, IDX_DIM)
    dsa_kv_s, cmp_kv_s, sel_kv_s = kv5(akv_s, 1), kv5(ckv_s, 1), kv5(skv_s, 1)
    win_kv_s = jnp.concatenate([state_nsa_win_kv, kv5(wkv_s, 1)], axis=1)[:, -nbuf:]
    dsa_idx_s = ik_s.reshape(bs, 1, IDX_DIM)

    g, b = ln_g[1], ln_b[1]
    yp = ffn_ln(yp, *ffn_w[1][0], g[0], b[0], tm)
    ys = ffn_ln(ys, *ffn_w[1][0], g[0], b[0], bs)

    w_mla_out_b = mla_w_out.astype(BF16)
    mtab_c, mtab_s = _rope_tables(pos, C_ROPE, ROPE_LO, ROPE_LO + C_ROPE)
    w_down_r, uq, uk, uv = _prep_mla(mla_w_down, mla_w_uq, mla_w_uk, mla_w_uv)
    qcat, kcat, v, ckv_p, kpe_p = mla_proj(yp, w_down_r, uq, uk, uv, mla_g_q, mla_g_kv, mtab_c, mtab_s, tm)
    o_m = mla_attn(qcat, kcat, v, batch, seq, tq_m, tk_m)
    yp = proj_ln(yp, [o_m], [w_mla_out_b], g[1], b[1], tm)

    smtab_c, smtab_s = _rope_tables(spos, C_ROPE, ROPE_LO, ROPE_LO + C_ROPE)
    qcat_s, _, _, ckv_s1, kpe_s1 = mla_proj(ys, w_down_r, uq, uk, uv, mla_g_q, mla_g_kv, smtab_c, smtab_s, bs)
    qh = qcat_s.reshape(bs, C_HEADS, LANES)
    q_nope, q_pe = qh[:, :, 0:C_NOPE], qh[:, :, ROPE_LO:ROPE_LO + C_ROPE]
    q_lat = head_bmm(q_nope.transpose(1, 0, 2), mla_w_uk.transpose(1, 2, 0)).transpose(1, 0, 2)
    o_lat = mla_sample_attn(page_table, q_lat, q_pe, row3(ckv_s1), row3(kpe_s1), cache_mla_ckv, cache_mla_kpe, ppc)
    o_s = head_bmm(o_lat.transpose(1, 0, 2), mla_w_uv.transpose(1, 0, 2)).transpose(1, 0, 2)
    ys = proj_ln(ys, [o_s.reshape(bs, C_HEADS * C_VDIM)], [w_mla_out_b], g[1], b[1], bs)

    yp = ffn_ln(yp, *ffn_w[1][1], g[2], b[2], tm)
    ys = ffn_ln(ys, *ffn_w[1][1], g[2], b[2], bs)

    return (yp.reshape(batch, seq, d), ys.reshape(bs, 1, d), dsa_kv_p, dsa_kv_s, dsa_idx_p, dsa_idx_s,
            cmp_kv_p, cmp_kv_s, sel_kv_p, sel_kv_s, win_kv_p, win_kv_s,
            ckv_p.reshape(batch, seq, C_KV_RANK), ckv_s1.reshape(bs, 1, C_KV_RANK),
            kpe_p.reshape(batch, seq, C_ROPE), kpe_s1.reshape(bs, 1, C_ROPE))
```

```python
import functools

import jax
import jax.numpy as jnp
import numpy as np
from jax import lax
from jax.experimental import pallas as pl
from jax.experimental.pallas import tpu as pltpu

F32 = jnp.float32
BF16 = jnp.bfloat16
I32 = jnp.int32

HEAD_DIM = 64
ROPE_THETA = 10000.0
PAGE_SIZE = 128
A_HEADS, A_KV_HEADS = 8, 2
IDX_HEADS, IDX_DIM = 8, 64
DSA_TOPK = 256
B_HEADS, B_KV_HEADS = 8, 2
CMP_BLOCK = 64
SEL_BLOCK = 64
SEL_TOPN = 16
WINDOW = 512
C_HEADS, C_NOPE, C_ROPE, C_VDIM = 16, 64, 32, 64
C_Q_RANK, C_KV_RANK = 384, 256
LN_EPS = 1e-5
RMS_EPS = 1e-6
DEPTH = 2
DN_ALPHA = (2 * DEPTH) ** 0.25

LANES = 128
VMEM_LIMIT = 56 * 1024 * 1024
NEG = -1e30
INT_MIN = -2 ** 31

_NT = (((1,), (1,)), ((), ()))
_TN = (((0,), (0,)), ((), ()))


def _cparams(sem):
    return pltpu.CompilerParams(dimension_semantics=sem, vmem_limit_bytes=VMEM_LIMIT)


def _resident(shape):
    nd = len(shape)
    return pl.BlockSpec(shape, lambda *_: (0,) * nd, pipeline_mode=pl.Buffered(1))


def _layer_norm(y, g, b):
    mu = jnp.mean(y, axis=-1, keepdims=True)
    yc = y - mu
    var = jnp.mean(yc * yc, axis=-1, keepdims=True)
    return yc * lax.rsqrt(var + LN_EPS) * g + b


def _swap_rope(v, tab_c, tab_s, first, d_half):
    partner = jnp.where(first, pltpu.roll(v, LANES - d_half, 1), pltpu.roll(v, d_half, 1))
    return v * tab_c + partner * tab_s


def _ffn_ln_kernel(x_ref, wg_ref, wu_ref, wd_ref, g_ref, b_ref, o_ref, *, n_chunks, chunk):
    x = x_ref[...]
    xb = x.astype(BF16)
    acc = None
    for c in range(n_chunks):
        sl = slice(c * chunk, (c + 1) * chunk)
        gate = jnp.dot(xb, wg_ref[:, sl], preferred_element_type=F32)
        up = jnp.dot(xb, wu_ref[:, sl], preferred_element_type=F32)
        h = (gate * jax.nn.sigmoid(gate) * up).astype(BF16)
        part = jnp.dot(h, wd_ref[sl, :], preferred_element_type=F32)
        acc = part if acc is None else acc + part
    y = DN_ALPHA * x + 0.5 * acc
    o_ref[...] = _layer_norm(y, g_ref[...], b_ref[...])


def _prep_ffn(w_gate, w_up, w_down):
    f = w_gate.shape[1]
    fp = -(-f // (2 * LANES)) * (2 * LANES)
    pad = fp - f
    wg = jnp.pad(w_gate, ((0, 0), (0, pad))).astype(BF16)
    wu = jnp.pad(w_up, ((0, 0), (0, pad))).astype(BF16)
    wd = jnp.pad(w_down, ((0, pad), (0, 0))).astype(BF16)
    return wg, wu, wd


def ffn_ln(x, wg, wu, wd, g, b, tm):
    n, d = x.shape
    fp = wg.shape[1]
    n_chunks = 2
    kern = functools.partial(_ffn_ln_kernel, n_chunks=n_chunks, chunk=fp // n_chunks)
    return pl.pallas_call(
        kern,
        grid=(n // tm,),
        in_specs=[
            pl.BlockSpec((tm, d), lambda i: (i, 0)),
            _resident(wg.shape), _resident(wu.shape), _resident(wd.shape),
            _resident((1, d)), _resident((1, d)),
        ],
        out_specs=pl.BlockSpec((tm, d), lambda i: (i, 0)),
        out_shape=jax.ShapeDtypeStruct((n, d), F32),
        compiler_params=_cparams(("parallel",)),
        name="ffn_ln",
    )(x, wg, wu, wd, g.reshape(1, d), b.reshape(1, d))


def _proj_ln_kernel(*refs, n_in):
    x_ref = refs[0]
    o_refs = refs[1:1 + n_in]
    w_refs = refs[1 + n_in:1 + 2 * n_in]
    g_ref, b_ref, y_ref = refs[1 + 2 * n_in:]
    acc = None
    for o_ref, w_ref in zip(o_refs, w_refs):
        part = jnp.dot(o_ref[...], w_ref[...], preferred_element_type=F32)
        acc = part if acc is None else acc + part
    y_ref[...] = _layer_norm(DN_ALPHA * x_ref[...] + acc, g_ref[...], b_ref[...])


def proj_ln(x, outs, ws, g, b, tm):
    n, d = x.shape
    n_in = len(outs)
    kern = functools.partial(_proj_ln_kernel, n_in=n_in)
    in_specs = [pl.BlockSpec((tm, d), lambda i: (i, 0))]
    in_specs += [pl.BlockSpec((tm, o.shape[1]), lambda i: (i, 0)) for o in outs]
    in_specs += [_resident(w.shape) for w in ws]
    in_specs += [_resident((1, d)), _resident((1, d))]
    return pl.pallas_call(
        kern,
        grid=(n // tm,),
        in_specs=in_specs,
        out_specs=pl.BlockSpec((tm, d), lambda i: (i, 0)),
        out_shape=jax.ShapeDtypeStruct((n, d), F32),
        compiler_params=_cparams(("parallel",)),
        name="proj_ln",
    )(x, *outs, *ws, g.reshape(1, d), b.reshape(1, d))


AB_Q_W = 3 * 512
AB_COLS = AB_Q_W + 4 * 256 + 128 + 128
MISC_IW = 0
MISC_GATE = IDX_HEADS


def _ab_column_perm():
    widths = (512, 128, 128, 512, 64, 8, 512, 128, 128, 128, 128, 128, 128, 24)
    offs = np.concatenate([[0], np.cumsum(widths)])
    seg = lambda i: np.arange(offs[i], offs[i + 1])
    (aq, ak, av, iq, ik, iw, bq, ck, cv, sk, sv, wk, wv, bg) = [seg(i) for i in range(14)]
    pad = lambda n: -np.ones(n, np.int64)
    return np.concatenate([aq, iq, bq, ak, av, ck, cv, sk, sv, wk, wv, ik, pad(64), iw, bg, pad(96)])


def _prep_ab_w_in(w_in):
    perm = _ab_column_perm()
    w = jnp.where((perm >= 0)[None, :], jnp.take(w_in, np.maximum(perm, 0), axis=1), 0.0)
    return w.astype(BF16)


def _rope_tables(pos, d, lane_lo, lane_hi):
    inv = jnp.power(ROPE_THETA, -jnp.arange(0, d, 2, dtype=F32) / d)
    ang = pos.astype(F32)[:, None] * inv[None, :]
    cos, sin = jnp.cos(ang), jnp.sin(ang)
    lane = np.arange(LANES)
    inside = (lane >= lane_lo) & (lane < lane_hi)
    j = (lane - lane_lo) % d
    idx = j % (d // 2)
    sign = np.where(j < d // 2, -1.0, 1.0).astype(np.float32)
    tab_c = jnp.where(inside[None, :], cos[:, idx], 1.0)
    tab_s = jnp.where(inside[None, :], sin[:, idx] * sign[None, :], 0.0)
    return tab_c, tab_s


def _ab_proj_kernel(x_ref, w_ref, c_ref, s_ref, q_ref, akv_ref, ckv_ref, skv_ref, wkv_ref, ik_ref, misc_ref):
    xb = x_ref[...].astype(BF16)
    tab_c, tab_s = c_ref[...], s_ref[...]
    lane = lax.broadcasted_iota(I32, tab_c.shape, 1)
    first = (lane % HEAD_DIM) < HEAD_DIM // 2
    rope = lambda v: _swap_rope(v, tab_c, tab_s, first, HEAD_DIM // 2)

    for blk in range(3):
        v = jnp.dot(xb, w_ref[:, blk * 512:(blk + 1) * 512], preferred_element_type=F32)
        for j in range(4):
            q_ref[:, blk * 512 + j * LANES: blk * 512 + (j + 1) * LANES] = (
                rope(v[:, j * LANES:(j + 1) * LANES]) * HEAD_DIM ** -0.5).astype(BF16)
    for n, kv_ref in enumerate((akv_ref, ckv_ref, skv_ref, wkv_ref)):
        c0 = AB_Q_W + n * 256
        v = jnp.dot(xb, w_ref[:, c0:c0 + 256], preferred_element_type=F32)
        kv_ref[:, 0:LANES] = rope(v[:, 0:LANES])
        kv_ref[:, LANES:2 * LANES] = v[:, LANES:2 * LANES]
    c0 = AB_Q_W + 4 * 256
    v = jnp.dot(xb, w_ref[:, c0:c0 + 256], preferred_element_type=F32)
    ik_ref[...] = rope(v[:, 0:LANES])[:, 0:IDX_DIM]
    m = v[:, LANES:2 * LANES]
    mlane = lax.broadcasted_iota(I32, m.shape, 1)
    misc_ref[...] = jnp.where(mlane < IDX_HEADS, m * IDX_HEADS ** -0.5, jax.nn.sigmoid(m))


def ab_proj(x, w_in_b, tab_c, tab_s, tm):
    n, d = x.shape
    n_tab = tab_c.shape[0] // tm
    row = lambda i: (i, 0)
    tab = lambda i: (i % n_tab, 0)
    out_shape = (
        jax.ShapeDtypeStruct((n, AB_Q_W), BF16),
        jax.ShapeDtypeStruct((n, 256), F32), jax.ShapeDtypeStruct((n, 256), F32),
        jax.ShapeDtypeStruct((n, 256), F32), jax.ShapeDtypeStruct((n, 256), F32),
        jax.ShapeDtypeStruct((n, IDX_DIM), F32),
        jax.ShapeDtypeStruct((n, LANES), F32),
    )
    out_specs = (
        pl.BlockSpec((tm, AB_Q_W), row),
        pl.BlockSpec((tm, 256), row), pl.BlockSpec((tm, 256), row),
        pl.BlockSpec((tm, 256), row), pl.BlockSpec((tm, 256), row),
        pl.BlockSpec((tm, IDX_DIM), row),
        pl.BlockSpec((tm, LANES), row),
    )
    return pl.pallas_call(
        _ab_proj_kernel,
        grid=(n // tm,),
        in_specs=[pl.BlockSpec((tm, d), row), _resident(w_in_b.shape),
                  pl.BlockSpec((tm, LANES), tab), pl.BlockSpec((tm, LANES), tab)],
        out_specs=out_specs,
        out_shape=out_shape,
        compiler_params=_cparams(("parallel",)),
        name="ab_proj",
    )(x, w_in_b, tab_c, tab_s)


def _sortable_key(s):
    bits = pltpu.bitcast(s, I32)
    return bits ^ ((bits >> 31) & 0x7FFFFFFF)


def _kth_threshold(count, k, shape, idx_bits):
    def tau_bit(b, tau):
        cand = tau + lax.shift_left(jnp.int32(1), jnp.asarray(31 - b, I32))
        return jnp.where(count(lambda key, idx: key >= cand) >= k, cand, tau)

    tau = lax.fori_loop(0, 32, tau_bit, jnp.full(shape, INT_MIN, I32))
    need = k - count(lambda key, idx: key > tau)

    def jlim_bit(b, p):
        cand = p + lax.shift_left(jnp.int32(1), jnp.asarray(idx_bits - 1 - b, I32))
        return jnp.where(count(lambda key, idx: (key == tau) & (idx < cand)) < need, cand, p)

    tied = count(lambda key, idx: key == tau)
    split = jnp.max(jnp.where(tied > need, 1.0, 0.0)) > 0.0
    jlim = lax.cond(split,
                    lambda: lax.fori_loop(0, idx_bits, jlim_bit, jnp.zeros(shape, I32)),
                    lambda: jnp.full(shape, (1 << idx_bits) - 1, I32))
    return tau, jlim


def _online_softmax_step(carry, logits, v):
    m, l, acc = carry
    m_new = jnp.maximum(m, jnp.max(logits, axis=1, keepdims=True))
    alpha = jnp.exp(m - m_new)
    p = jnp.exp(logits - _lanes_to(m_new, logits.shape[1]))
    l_new = alpha * l + jnp.sum(p, axis=1, keepdims=True)
    acc_new = _lanes_to(alpha, acc.shape[1]) * acc + jnp.dot(p.astype(BF16), v, preferred_element_type=F32)
    return m_new, l_new, acc_new


def _lanes_to(x, width):
    return x[:, 0:width] if width <= LANES else jnp.concatenate([x] * (width // LANES), axis=1)


def _softmax_init(rows, dv):
    return jnp.full((rows, LANES), NEG, F32), jnp.zeros((rows, LANES), F32), jnp.zeros((rows, dv), F32)


def _stack_heads(q, g, per_group):
    return jnp.concatenate(
        [q[:, (g * per_group + h) * HEAD_DIM:(g * per_group + h + 1) * HEAD_DIM] for h in range(per_group)], axis=0)


def _dsa_prompt_kernel(q_ref, misc_ref, ik_ref, akv_ref, o_ref, key_scr, *, tq, tk, topk, seq):
    i = pl.program_id(1)
    q0 = i * tq
    n_chunks = (q0 + tq + tk - 1) // tk
    row = q0 + lax.broadcasted_iota(I32, (tq, tk), 0)
    lane_col = lax.broadcasted_iota(I32, (tq, tk), 1)
    lane_tile = lax.broadcasted_iota(I32, (tq, LANES), 1)

    iq = q_ref[:, 512:1024]
    w = misc_ref[:, MISC_IW:MISC_IW + IDX_HEADS]

    def score_chunk(c, _):
        k0 = pl.multiple_of(c * tk, tk)
        kc = ik_ref[pl.ds(k0, tk), :].astype(BF16)
        s = jnp.zeros((tq, tk), F32)
        for h in range(IDX_HEADS):
            d = lax.dot_general(iq[:, h * IDX_DIM:(h + 1) * IDX_DIM], kc, _NT, preferred_element_type=F32)
            s = s + jnp.maximum(d, 0.0) * w[:, h:h + 1]
        key = jnp.where(k0 + lane_col <= row, _sortable_key(s), INT_MIN)
        key_scr[c] = key
        return 0

    lax.fori_loop(0, n_chunks, score_chunk, 0)

    def count(pred_fn):
        def body(c, acc):
            k0 = c * tk
            for j in range(tk // LANES):
                sl = slice(j * LANES, (j + 1) * LANES)
                acc = acc + jnp.where(pred_fn(key_scr[c, :, sl], k0 + j * LANES + lane_tile), 1.0, 0.0)
            return acc
        acc = lax.fori_loop(0, n_chunks, body, jnp.zeros((tq, LANES), F32))
        return jnp.broadcast_to(jnp.sum(acc, axis=1, keepdims=True), (tq, LANES))

    tau, jlim = _kth_threshold(count, topk, (tq, LANES), max(1, int(seq - 1).bit_length()))

    tau_c = jnp.broadcast_to(tau[:, 0:1], (tq, tk))
    jlim_c = jnp.broadcast_to(jlim[:, 0:1], (tq, tk))

    aq = q_ref[:, 0:512]
    per_group = A_HEADS // A_KV_HEADS
    qg = [_stack_heads(aq, g, per_group) for g in range(A_KV_HEADS)]

    def attn_chunk(c, carry):
        k0 = pl.multiple_of(c * tk, tk)
        key = key_scr[c]
        col = k0 + lane_col
        sel = ((key > tau_c) | ((key == tau_c) & (col <= jlim_c))) & (col <= row)
        kv = akv_ref[pl.ds(k0, tk), :]
        out = []
        for g in range(A_KV_HEADS):
            k = kv[:, g * HEAD_DIM:(g + 1) * HEAD_DIM].astype(BF16)
            v = kv[:, LANES + g * HEAD_DIM:LANES + (g + 1) * HEAD_DIM].astype(BF16)
            logits = lax.dot_general(qg[g], k, _NT, preferred_element_type=F32)
            logits = jnp.where(sel[None], logits.reshape(per_group, tq, tk), NEG).reshape(per_group * tq, tk)
            out.append(_online_softmax_step(carry[g], logits, v))
        return tuple(out)

    init = tuple(_softmax_init(per_group * tq, HEAD_DIM) for _ in range(A_KV_HEADS))
    res = lax.fori_loop(0, n_chunks, attn_chunk, init)
    heads = []
    for g in range(A_KV_HEADS):
        _, l, acc = res[g]
        o = acc / l[:, 0:HEAD_DIM]
        heads += [o[h * tq:(h + 1) * tq] for h in range(per_group)]
    o_ref[...] = jnp.concatenate(heads, axis=1).astype(BF16)


def dsa_prompt(q_all, misc, ik, akv, batch, seq, tq, tk):
    n = batch * seq
    nq = seq // tq
    kern = functools.partial(_dsa_prompt_kernel, tq=tq, tk=tk, topk=min(DSA_TOPK, seq // 4), seq=seq)
    qrow = lambda b, i: (b * nq + i, 0)
    return pl.pallas_call(
        kern,
        grid=(batch, nq),
        in_specs=[pl.BlockSpec((tq, AB_Q_W), qrow), pl.BlockSpec((tq, LANES), qrow),
                  pl.BlockSpec((seq, IDX_DIM), lambda b, i: (b, 0)),
                  pl.BlockSpec((seq, 256), lambda b, i: (b, 0))],
        out_specs=pl.BlockSpec((tq, A_HEADS * HEAD_DIM), qrow),
        out_shape=jax.ShapeDtypeStruct((n, A_HEADS * HEAD_DIM), BF16),
        scratch_shapes=[pltpu.VMEM((seq // tk, tq, tk), I32)],
        compiler_params=_cparams(("parallel", "parallel")),
        name="dsa_prompt",
    )(q_all, misc, ik, akv)


def _nsa_prompt_kernel(q_ref, misc_ref, ckv_ref, skv_ref, wkv_ref, o_ref, kc_scr, *, tq, tk, seq, wlen):
    i = pl.program_id(1)
    q0 = i * tq
    nc = seq // CMP_BLOCK
    n_top = min(SEL_TOPN, -(-seq // SEL_BLOCK))
    per_group = B_HEADS // B_KV_HEADS
    n_chunks = (q0 + tq + tk - 1) // tk

    @pl.when(i == 0)
    def _():
        rows = 8 * CMP_BLOCK
        def body(j, _):
            r0 = pl.multiple_of(j * rows, rows)
            x = ckv_ref[pl.ds(r0, rows), :].reshape(8, CMP_BLOCK, 256)
            kc_scr[pl.ds(pl.multiple_of(j * 8, 8), 8), :] = jnp.sum(x, axis=1) * (1.0 / CMP_BLOCK)
            return 0
        lax.fori_loop(0, seq // rows, body, 0)

    bq = q_ref[:, 1024:1536]
    gates = misc_ref[:, MISC_GATE:MISC_GATE + 3 * B_HEADS]
    kc = kc_scr[...]

    blk_t = lax.broadcasted_iota(I32, (nc, tq), 0)
    qpos_t = q0 + lax.broadcasted_iota(I32, (nc, tq), 1)
    visible = (blk_t + 1) * CMP_BLOCK - 1 <= qpos_t
    cur_t = qpos_t // SEL_BLOCK
    o_c, sel_t = [], []
    for g in range(B_KV_HEADS):
        kck = kc[:, g * HEAD_DIM:(g + 1) * HEAD_DIM].astype(BF16)
        kcv = kc[:, LANES + g * HEAD_DIM:LANES + (g + 1) * HEAD_DIM].astype(BF16)
        imp = jnp.zeros((nc, tq), F32)
        for h in range(per_group):
            hh = g * per_group + h
            qh = bq[:, hh * HEAD_DIM:(hh + 1) * HEAD_DIM]
            lc = lax.dot_general(kck, qh, _NT, preferred_element_type=F32)
            lc = jnp.where(visible, lc, -jnp.inf)
            m = jnp.max(lc, axis=0, keepdims=True)
            e = jnp.exp(lc - jnp.where(m > -jnp.inf, m, 0.0))
            ssum = jnp.sum(e, axis=0, keepdims=True)
            pc = e / jnp.where(ssum > 0, ssum, 1.0)
            imp = imp + pc
            o_c.append(lax.dot_general(pc.astype(BF16), kcv, _TN, preferred_element_type=F32))
        imp = jnp.where((blk_t == 0) | (blk_t == cur_t), jnp.inf, imp)
        imp = jnp.where(blk_t > cur_t, -jnp.inf, imp)
        rank = jnp.zeros((nc, tq), F32)
        for j in range(nc):
            r = imp[j:j + 1, :]
            rank = rank + jnp.where((r > imp) | ((r == imp) & (blk_t > j)), 1.0, 0.0)
        sel_t.append(jnp.where(rank < n_top, 1.0, 0.0).astype(BF16))

    qg = [_stack_heads(bq, g, per_group) for g in range(B_KV_HEADS)]
    row = q0 + lax.broadcasted_iota(I32, (tq, tk), 0)
    lane_col = lax.broadcasted_iota(I32, (tq, tk), 1)
    blk_e = lax.broadcasted_iota(I32, (nc, tk), 0)
    col_e = lax.broadcasted_iota(I32, (nc, tk), 1)

    def sel_chunk(c, carry):
        k0 = pl.multiple_of(c * tk, tk)
        expand = jnp.where((k0 + col_e) // SEL_BLOCK == blk_e, 1.0, 0.0).astype(BF16)
        causal = k0 + lane_col <= row
        kv = skv_ref[pl.ds(k0, tk), :]
        out = []
        for g in range(B_KV_HEADS):
            picked = lax.dot_general(sel_t[g], expand, _TN, preferred_element_type=F32) > 0.5
            k = kv[:, g * HEAD_DIM:(g + 1) * HEAD_DIM].astype(BF16)
            v = kv[:, LANES + g * HEAD_DIM:LANES + (g + 1) * HEAD_DIM].astype(BF16)
            logits = lax.dot_general(qg[g], k, _NT, preferred_element_type=F32)
            logits = jnp.where((picked & causal)[None], logits.reshape(per_group, tq, tk), NEG)
            out.append(_online_softmax_step(carry[g], logits.reshape(per_group * tq, tk), v))
        return tuple(out)

    init = tuple(_softmax_init(per_group * tq, HEAD_DIM) for _ in range(B_KV_HEADS))
    res = lax.fori_loop(0, n_chunks, sel_chunk, init)
    o_s = []
    for g in range(B_KV_HEADS):
        _, l, acc = res[g]
        o = acc / l[:, 0:HEAD_DIM]
        o_s += [o[h * tq:(h + 1) * tq] for h in range(per_group)]

    ws = pl.multiple_of(jnp.maximum(q0 + tq - wlen, 0), tq)
    wkv = wkv_ref[pl.ds(ws, wlen), :]
    dist = (q0 + lax.broadcasted_iota(I32, (tq, wlen), 0)) - (ws + lax.broadcasted_iota(I32, (tq, wlen), 1))
    in_win = (dist >= 0) & (dist <= WINDOW)
    o_w = []
    for g in range(B_KV_HEADS):
        k = wkv[:, g * HEAD_DIM:(g + 1) * HEAD_DIM].astype(BF16)
        v = wkv[:, LANES + g * HEAD_DIM:LANES + (g + 1) * HEAD_DIM].astype(BF16)
        logits = lax.dot_general(qg[g], k, _NT, preferred_element_type=F32)
        logits = jnp.where(in_win[None], logits.reshape(per_group, tq, wlen), NEG).reshape(per_group * tq, wlen)
        m = jnp.max(logits, axis=1, keepdims=True)
        p = jnp.exp(logits - m)
        o = jnp.dot(p.astype(BF16), v, preferred_element_type=F32) / jnp.sum(p, axis=1, keepdims=True)
        o_w += [o[h * tq:(h + 1) * tq] for h in range(per_group)]

    heads = []
    for h in range(B_HEADS):
        gate = lambda br: gates[:, br * B_HEADS + h: br * B_HEADS + h + 1]
        heads.append(gate(0) * o_c[h] + gate(1) * o_s[h] + gate(2) * o_w[h])
    o_ref[...] = jnp.concatenate(heads, axis=1).astype(BF16)


def nsa_prompt(q_all, misc, ckv, skv, wkv, batch, seq, tq, tk):
    n = batch * seq
    nq = seq // tq
    wlen = min(WINDOW + tq, seq)
    kern = functools.partial(_nsa_prompt_kernel, tq=tq, tk=tk, seq=seq, wlen=wlen)
    qrow = lambda b, i: (b * nq + i, 0)
    full = lambda b, i: (b, 0)
    return pl.pallas_call(
        kern,
        grid=(batch, nq),
        in_specs=[pl.BlockSpec((tq, AB_Q_W), qrow), pl.BlockSpec((tq, LANES), qrow),
                  pl.BlockSpec((seq, 256), full), pl.BlockSpec((seq, 256), full), pl.BlockSpec((seq, 256), full)],
        out_specs=pl.BlockSpec((tq, B_HEADS * HEAD_DIM), qrow),
        out_shape=jax.ShapeDtypeStruct((n, B_HEADS * HEAD_DIM), BF16),
        scratch_shapes=[pltpu.VMEM((seq // CMP_BLOCK, 256), F32)],
        compiler_params=_cparams(("parallel", "arbitrary")),
        name="nsa_prompt",
    )(q_all, misc, ckv, skv, wkv)


MLA_CAT = C_HEADS * LANES
MLA_DOWN_COLS = C_Q_RANK + C_KV_RANK + LANES
ROPE_LO = C_NOPE


def _prep_mla(w_down, w_uq, w_uk, w_uv):
    cq, ckv, kpe = jnp.split(w_down, [C_Q_RANK, C_Q_RANK + C_KV_RANK], axis=1)
    z = lambda n: jnp.zeros((w_down.shape[0], n), w_down.dtype)
    w_down_r = jnp.concatenate([cq, ckv, z(ROPE_LO), kpe, z(LANES - ROPE_LO - C_ROPE)], axis=1).astype(BF16)
    uq = w_uq.reshape(C_Q_RANK, C_HEADS, C_NOPE + C_ROPE)
    uq = jnp.pad(uq, ((0, 0), (0, 0), (0, LANES - C_NOPE - C_ROPE))).reshape(C_Q_RANK, MLA_CAT).astype(BF16)
    uk = jnp.pad(w_uk, ((0, 0), (0, 0), (0, LANES - C_NOPE))).reshape(C_KV_RANK, MLA_CAT).astype(BF16)
    uv = w_uv.reshape(C_KV_RANK, C_HEADS * C_VDIM).astype(BF16)
    return w_down_r, uq, uk, uv


def _rms(x, g):
    return x * lax.rsqrt(jnp.mean(x * x, axis=-1, keepdims=True) + RMS_EPS) * g


def _mla_proj_kernel(x_ref, wd_ref, uq_ref, uk_ref, uv_ref, gq_ref, gkv_ref, c_ref, s_ref,
                     qcat_ref, kcat_ref, v_ref, ckv_ref, kpe_ref):
    xb = x_ref[...].astype(BF16)
    tab_c, tab_s = c_ref[...], s_ref[...]
    lane = lax.broadcasted_iota(I32, tab_c.shape, 1)
    first = (lane % C_ROPE) < C_ROPE // 2
    rope = lambda v: _swap_rope(v, tab_c, tab_s, first, C_ROPE // 2)

    c = jnp.dot(xb, wd_ref[...], preferred_element_type=F32)
    cq = _rms(c[:, 0:C_Q_RANK], gq_ref[...]).astype(BF16)
    ckv = _rms(c[:, C_Q_RANK:C_Q_RANK + C_KV_RANK], gkv_ref[...])
    kpe = rope(c[:, C_Q_RANK + C_KV_RANK:])
    ckv_ref[...] = ckv
    kpe_ref[...] = kpe[:, ROPE_LO:ROPE_LO + C_ROPE]
    ckv_b = ckv.astype(BF16)
    v_ref[...] = jnp.dot(ckv_b, uv_ref[...], preferred_element_type=F32).astype(BF16)
    for h in range(C_HEADS):
        sl = slice(h * LANES, (h + 1) * LANES)
        q = jnp.dot(cq, uq_ref[:, sl], preferred_element_type=F32)
        qcat_ref[:, sl] = rope(q).astype(BF16)
        k = jnp.dot(ckv_b, uk_ref[:, sl], preferred_element_type=F32)
        kcat_ref[:, sl] = (k + kpe).astype(BF16)


def mla_proj(x, w_down_r, uq, uk, uv, g_q, g_kv, tab_c, tab_s, tm):
    n, d = x.shape
    n_tab = tab_c.shape[0] // tm
    row = lambda i: (i, 0)
    tab = lambda i: (i % n_tab, 0)
    out_shape = (
        jax.ShapeDtypeStruct((n, MLA_CAT), BF16), jax.ShapeDtypeStruct((n, MLA_CAT), BF16),
        jax.ShapeDtypeStruct((n, C_HEADS * C_VDIM), BF16),
        jax.ShapeDtypeStruct((n, C_KV_RANK), F32), jax.ShapeDtypeStruct((n, C_ROPE), F32),
    )
    out_specs = (
        pl.BlockSpec((tm, MLA_CAT), row), pl.BlockSpec((tm, MLA_CAT), row),
        pl.BlockSpec((tm, C_HEADS * C_VDIM), row),
        pl.BlockSpec((tm, C_KV_RANK), row), pl.BlockSpec((tm, C_ROPE), row),
    )
    return pl.pallas_call(
        _mla_proj_kernel,
        grid=(n // tm,),
        in_specs=[pl.BlockSpec((tm, d), row), _resident(w_down_r.shape), _resident(uq.shape),
                  _resident(uk.shape), _resident(uv.shape),
                  _resident((1, C_Q_RANK)), _resident((1, C_KV_RANK)),
                  pl.BlockSpec((tm, LANES), tab), pl.BlockSpec((tm, LANES), tab)],
        out_specs=out_specs,
        out_shape=out_shape,
        compiler_params=_cparams(("parallel",)),
        name="mla_proj",
    )(x, w_down_r, uq, uk, uv, g_q.reshape(1, -1), g_kv.reshape(1, -1), tab_c, tab_s)


def _mla_attn_kernel(q_ref, k_ref, v_ref, o_ref, m_scr, l_scr, acc_scr, *, tq, tk):
    i, j = pl.program_id(1), pl.program_id(2)
    scale = (C_NOPE + C_ROPE) ** -0.5

    @pl.when(j == 0)
    def _():
        m_scr[...] = jnp.full(m_scr.shape, NEG, F32)
        l_scr[...] = jnp.zeros(l_scr.shape, F32)
        acc_scr[...] = jnp.zeros(acc_scr.shape, F32)

    def update(on_diagonal):
        if on_diagonal:
            row = i * tq + lax.broadcasted_iota(I32, (tq, tk), 0)
            col = j * tk + lax.broadcasted_iota(I32, (tq, tk), 1)
            causal = col <= row
        for h in range(C_HEADS):
            q = q_ref[:, h * LANES:(h + 1) * LANES]
            k = k_ref[:, h * LANES:(h + 1) * LANES]
            v = v_ref[:, h * C_VDIM:(h + 1) * C_VDIM]
            logits = lax.dot_general(q, k, _NT, preferred_element_type=F32) * scale
            if on_diagonal:
                logits = jnp.where(causal, logits, NEG)
            m_scr[h], l_scr[h], acc_scr[h] = _online_softmax_step((m_scr[h], l_scr[h], acc_scr[h]), logits, v)

    needed = j * tk <= i * tq + tq - 1
    crosses = j * tk + tk - 1 > i * tq
    pl.when(needed & crosses)(lambda: update(True))
    pl.when(needed & jnp.logical_not(crosses))(lambda: update(False))

    @pl.when(j == pl.num_programs(2) - 1)
    def _():
        o_ref[...] = jnp.concatenate(
            [acc_scr[h] / l_scr[h][:, 0:C_VDIM] for h in range(C_HEADS)], axis=1).astype(BF16)


def mla_attn(qcat, kcat, v, batch, seq, tq, tk):
    n = batch * seq
    nq, nk = seq // tq, seq // tk
    kern = functools.partial(_mla_attn_kernel, tq=tq, tk=tk)
    last = lambda i: (i * tq + tq - 1) // tk
    kv_row = lambda b, i, j: (b * nk + jnp.minimum(j, last(i)), 0)
    return pl.pallas_call(
        kern,
        grid=(batch, nq, nk),
        in_specs=[pl.BlockSpec((tq, MLA_CAT), lambda b, i, j: (b * nq + i, 0)),
                  pl.BlockSpec((tk, MLA_CAT), kv_row),
                  pl.BlockSpec((tk, C_HEADS * C_VDIM), kv_row)],
        out_specs=pl.BlockSpec((tq, C_HEADS * C_VDIM), lambda b, i, j: (b * nq + i, 0)),
        out_shape=jax.ShapeDtypeStruct((n, C_HEADS * C_VDIM), BF16),
        scratch_shapes=[pltpu.VMEM((C_HEADS, tq, LANES), F32), pltpu.VMEM((C_HEADS, tq, LANES), F32),
                        pltpu.VMEM((C_HEADS, tq, C_VDIM), F32)],
        compiler_params=_cparams(("parallel", "parallel", "arbitrary")),
        name="mla_attn",
    )(qcat, kcat, v)


PAGES_PER_CHUNK = 16
Q_ROWS = 8


def _chunk_copies(pt_ref, pool_ref, buf_ref, sem, b, c, slot, ppc, rows_on_lanes):
    def dst(p):
        win = pl.ds(p * PAGE_SIZE, PAGE_SIZE)
        return buf_ref.at[slot, :, win] if rows_on_lanes else buf_ref.at[slot, win]
    return [pltpu.make_async_copy(pool_ref.at[pt_ref[b, c * ppc + p]], dst(p), sem.at[slot]) for p in range(ppc)]


def _stream_chunk(pt_ref, pools, bufs, sems, ppc, rows_on_lanes=None):
    b, c = pl.program_id(0), pl.program_id(1)
    n_c = pl.num_programs(1)
    step = b * n_c + c
    slot = step % 2
    rol = rows_on_lanes or (True,) * len(pools)

    def copies(bb, cc, s):
        return [cp for pool, buf, sem, r in zip(pools, bufs, sems, rol)
                for cp in _chunk_copies(pt_ref, pool, buf, sem, bb, cc, s, ppc, r)]

    def start(bb, cc, s):
        for cp in copies(bb, cc, s):
            cp.start()

    @pl.when(step == 0)
    def _():
        start(b, c, slot)

    @pl.when(step + 1 < pl.num_programs(0) * n_c)
    def _():
        wrap = c + 1 == n_c
        start(jnp.where(wrap, b + 1, b), jnp.where(wrap, 0, c + 1), 1 - slot)

    for cp in copies(b, c, slot):
        cp.wait()
    return slot


def _paged_call(kern, name, page_table, n_chunks, in_arrays, in_specs, pools, out_shape, out_specs, scratch):
    bs = page_table.shape[0]
    any_spec = pl.BlockSpec(memory_space=pl.ANY)
    return pl.pallas_call(
        kern,
        grid_spec=pltpu.PrefetchScalarGridSpec(
            num_scalar_prefetch=1, grid=(bs, n_chunks),
            in_specs=list(in_specs) + [any_spec] * len(pools),
            out_specs=out_specs, scratch_shapes=scratch),
        out_shape=out_shape,
        compiler_params=_cparams(("arbitrary", "arbitrary")),
        name=name,
    )(page_table, *in_arrays, *pools)


def _per_seq(shape):
    nd = len(shape)
    return pl.BlockSpec((1,) + tuple(shape), lambda b, c, pt: (b,) + (0,) * nd)


def _group_rows(x0, x1):
    r = lax.broadcasted_iota(I32, x0.shape, 0)
    return jnp.where(r < Q_ROWS // 2, x0, x1)


def _new_row_terms(q, new_row):
    nb = new_row.astype(BF16).astype(F32)
    bc = lambda a: jnp.broadcast_to(a, (Q_ROWS, HEAD_DIM))
    k = _group_rows(bc(nb[:, 0:HEAD_DIM]), bc(nb[:, HEAD_DIM:2 * HEAD_DIM]))
    v = _group_rows(bc(nb[:, LANES:LANES + HEAD_DIM]), bc(nb[:, LANES + HEAD_DIM:2 * LANES]))
    return jnp.sum(q.astype(F32) * k, axis=1, keepdims=True), v


def _gqa_decode_chunk(q, kv, mask, carry):
    kvb = kv.astype(BF16)
    lg = [jnp.dot(q, kvb[g * HEAD_DIM:(g + 1) * HEAD_DIM, :], preferred_element_type=F32) for g in range(2)]
    logits = jnp.where(mask, _group_rows(lg[0], lg[1]), NEG)
    m, l, acc = carry
    m_new = jnp.maximum(m, jnp.max(logits, axis=1, keepdims=True))
    alpha = jnp.exp(m - m_new)
    p = jnp.exp(logits - m_new)
    pb = p.astype(BF16)
    pv = [lax.dot_general(pb, kvb[LANES + g * HEAD_DIM:LANES + (g + 1) * HEAD_DIM, :], _NT,
                          preferred_element_type=F32) for g in range(2)]
    return m_new, alpha * l + jnp.sum(p, axis=1, keepdims=True), alpha * acc + _group_rows(pv[0], pv[1])


def _dsa_sample_index_kernel(pt_ref, iq_ref, w_ref, ikn_ref, pool_ref, keys_ref, meta_ref, buf, sem,
                             *, ppc, topk, past):
    c = pl.program_id(1)
    ck = ppc * PAGE_SIZE
    slot = _stream_chunk(pt_ref, [pool_ref], [buf], [sem], ppc)
    iq = iq_ref[0]
    w = w_ref[0]
    rows = keys_ref.shape[1]

    @pl.when(c == 0)
    def _():
        keys_ref[0] = jnp.full((rows, ck), INT_MIN, I32)

    d = jnp.dot(iq, buf[slot].astype(BF16), preferred_element_type=F32)
    s = jnp.sum(jnp.maximum(d, 0.0) * w, axis=0, keepdims=True)
    keys_ref[0, pl.ds(c, 1), :] = _sortable_key(s)

    @pl.when(c == pl.num_programs(1) - 1)
    def _():
        dn = jnp.sum(iq.astype(F32) * ikn_ref[0].astype(BF16).astype(F32), axis=1, keepdims=True)
        sn = jnp.sum(jnp.maximum(dn, 0.0) * w, axis=0, keepdims=True)
        key_new = _sortable_key(jnp.broadcast_to(sn, (rows, LANES)))
        idx_new = jnp.full((rows, LANES), past, I32)
        idx_tile = (lax.broadcasted_iota(I32, (rows, LANES), 0) * ck + lax.broadcasted_iota(I32, (rows, LANES), 1))

        def count(pred):
            acc = jnp.zeros((rows, LANES), F32)
            for j in range(ck // LANES):
                acc = acc + jnp.where(pred(keys_ref[0, :, j * LANES:(j + 1) * LANES], idx_tile + j * LANES), 1.0, 0.0)
            tot = jnp.sum(jnp.sum(acc, axis=1, keepdims=True), axis=0, keepdims=True)
            return jnp.broadcast_to(tot, (rows, LANES)) + jnp.where(pred(key_new, idx_new), 1.0, 0.0)

        tau, jlim = _kth_threshold(count, topk, (rows, LANES), int(past).bit_length())
        meta_ref[0, 0] = tau
        meta_ref[0, 1] = jlim
        meta_ref[0, 2] = key_new


def dsa_sample_index(page_table, iq, w, ik_new, idx_pool, ppc):
    bs, n_pages = page_table.shape
    n_c = n_pages // ppc
    ck = ppc * PAGE_SIZE
    past = n_pages * PAGE_SIZE
    rows = -(-n_c // 8) * 8
    kern = functools.partial(_dsa_sample_index_kernel, ppc=ppc, topk=min(DSA_TOPK, (past + 1) // 4), past=past)
    return _paged_call(
        kern, "dsa_sample_index", page_table, n_c, [iq, w, ik_new],
        [_per_seq((Q_ROWS, IDX_DIM)), _per_seq((Q_ROWS, 1)), _per_seq((1, IDX_DIM))], [idx_pool],
        (jax.ShapeDtypeStruct((bs, rows, ck), I32), jax.ShapeDtypeStruct((bs, 3, rows, LANES), I32)),
        (_per_seq((rows, ck)), _per_seq((3, rows, LANES))),
        [pltpu.VMEM((2, IDX_DIM, ck), F32), pltpu.SemaphoreType.DMA((2,))])


def _dsa_sample_attn_kernel(pt_ref, q_ref, new_ref, keys_ref, meta_ref, pool_ref, o_ref, buf, sem,
                            m_scr, l_scr, acc_scr, *, ppc, past):
    c = pl.program_id(1)
    ck = ppc * PAGE_SIZE
    slot = _stream_chunk(pt_ref, [pool_ref], [buf], [sem], ppc)
    q = q_ref[0]
    tau, jlim, key_new = meta_ref[0, 0], meta_ref[0, 1], meta_ref[0, 2]

    @pl.when(c == 0)
    def _():
        logit, v = _new_row_terms(q, new_ref[0])
        t8, j8, k8 = tau[0:Q_ROWS, 0:1], jlim[0:Q_ROWS, 0:1], key_new[0:Q_ROWS, 0:1]
        picked = (k8 > t8) | ((k8 == t8) & (j8 >= past))
        m_scr[...] = jnp.where(picked, logit, NEG)
        l_scr[...] = jnp.where(picked, 1.0, 0.0)
        acc_scr[...] = jnp.where(picked, v, 0.0)

    key = keys_ref[0, pl.ds(c, 1), :]
    idx = c * ck + lax.broadcasted_iota(I32, (1, ck), 1)
    t1, j1 = tau[0:1, 0:1], jlim[0:1, 0:1]
    sel = (key > t1) | ((key == t1) & (idx <= j1))
    m, l, acc = _gqa_decode_chunk(q, buf[slot], sel, (m_scr[...], l_scr[...], acc_scr[...]))
    m_scr[...], l_scr[...], acc_scr[...] = m, l, acc

    @pl.when(c == pl.num_programs(1) - 1)
    def _():
        o_ref[0] = (acc / l).astype(BF16)


def dsa_sample_attn(page_table, q, new_row, keys, meta, kv_pool, ppc):
    bs, n_pages = page_table.shape
    ck = ppc * PAGE_SIZE
    rows = keys.shape[1]
    kern = functools.partial(_dsa_sample_attn_kernel, ppc=ppc, past=n_pages * PAGE_SIZE)
    return _paged_call(
        kern, "dsa_sample_attn", page_table, n_pages // ppc, [q, new_row, keys, meta],
        [_per_seq((Q_ROWS, HEAD_DIM)), _per_seq((1, 256)), _per_seq((rows, ck)), _per_seq((3, rows, LANES))],
        [kv_pool],
        jax.ShapeDtypeStruct((bs, Q_ROWS, HEAD_DIM), BF16), _per_seq((Q_ROWS, HEAD_DIM)),
        [pltpu.VMEM((2, 256, ck), F32), pltpu.SemaphoreType.DMA((2,)),
         pltpu.VMEM((Q_ROWS, 1), F32), pltpu.VMEM((Q_ROWS, 1), F32), pltpu.VMEM((Q_ROWS, HEAD_DIM), F32)])


def _nsa_sample_cmp_kernel(pt_ref, q_ref, exp_ref, pool_ref, oc_ref, sel_ref, buf, sem, kc_scr,
                           *, ppc, past, n_keep):
    c = pl.program_id(1)
    n_c, _, bpc = kc_scr.shape
    nc = n_c * bpc
    slot = _stream_chunk(pt_ref, [pool_ref], [buf], [sem], ppc)
    x = buf[slot]
    hi = x.astype(BF16)
    lo = (x - hi.astype(F32)).astype(BF16)
    member = exp_ref[...]
    kc_scr[c] = (lax.dot_general(hi, member, _NT, preferred_element_type=F32)
                 + lax.dot_general(lo, member, _NT, preferred_element_type=F32)) * (1.0 / CMP_BLOCK)

    @pl.when(c == n_c - 1)
    def _():
        q = q_ref[0]
        kcb = [kc_scr[j].astype(BF16) for j in range(n_c)]
        lc = jnp.concatenate(
            [_group_rows(*[jnp.dot(q, kcb[j][g * HEAD_DIM:(g + 1) * HEAD_DIM, :], preferred_element_type=F32)
                           for g in range(2)]) for j in range(n_c)], axis=1)
        blk = lax.broadcasted_iota(I32, (Q_ROWS, nc), 1)
        row = lax.broadcasted_iota(I32, (Q_ROWS, nc), 0)
        lc = jnp.where((blk + 1) * CMP_BLOCK - 1 <= past, lc, -jnp.inf)
        m = jnp.max(lc, axis=1, keepdims=True)
        e = jnp.exp(lc - jnp.where(m > -jnp.inf, m, 0.0))
        ssum = jnp.sum(e, axis=1, keepdims=True)
        pc = e / jnp.where(ssum > 0, ssum, 1.0)
        pcb = pc.astype(BF16)
        o_c = jnp.zeros((Q_ROWS, HEAD_DIM), F32)
        for j in range(n_c):
            o_c = o_c + _group_rows(*[lax.dot_general(
                pcb[:, j * bpc:(j + 1) * bpc], kcb[j][LANES + g * HEAD_DIM:LANES + (g + 1) * HEAD_DIM, :], _NT,
                preferred_element_type=F32) for g in range(2)])
        oc_ref[0] = o_c
        imp0 = jnp.sum(jnp.where(row < Q_ROWS // 2, pc, 0.0), axis=0, keepdims=True)
        imp1 = jnp.sum(jnp.where(row >= Q_ROWS // 2, pc, 0.0), axis=0, keepdims=True)
        imp = jnp.where(row == 0, imp0, jnp.where(row == 1, imp1, -jnp.inf))
        imp = jnp.where((blk == 0) & (row < 2), jnp.inf, imp)
        key = _sortable_key(imp)
        blk_tile = lax.broadcasted_iota(I32, (Q_ROWS, LANES), 1)

        def count(pred):
            acc = jnp.zeros((Q_ROWS, LANES), F32)
            for j in range(nc // LANES):
                acc = acc + jnp.where(pred(key[:, j * LANES:(j + 1) * LANES], blk_tile + j * LANES), 1.0, 0.0)
            return jnp.broadcast_to(jnp.sum(acc, axis=1, keepdims=True), (Q_ROWS, LANES))

        tau, jlim = _kth_threshold(count, n_keep, (Q_ROWS, LANES), max(1, int(nc - 1).bit_length()))
        for j in range(nc // LANES):
            kj = key[:, j * LANES:(j + 1) * LANES]
            sel_ref[0, :, j * LANES:(j + 1) * LANES] = jnp.where(
                (kj > tau) | ((kj == tau) & (blk_tile + j * LANES <= jlim)), 1.0, 0.0)


def nsa_sample_cmp(page_table, q, expand, cmp_pool, ppc):
    bs, n_pages = page_table.shape
    past = n_pages * PAGE_SIZE
    nc = past // CMP_BLOCK
    ck = ppc * PAGE_SIZE
    bpc = ck // CMP_BLOCK
    n_blk = -(-(past + 1) // SEL_BLOCK)
    kern = functools.partial(_nsa_sample_cmp_kernel, ppc=ppc, past=past, n_keep=min(SEL_TOPN, n_blk) - 1)
    return _paged_call(
        kern, "nsa_sample_cmp", page_table, n_pages // ppc, [q, expand],
        [_per_seq((Q_ROWS, HEAD_DIM)), pl.BlockSpec((bpc, ck), lambda b, c, pt: (0, 0))], [cmp_pool],
        (jax.ShapeDtypeStruct((bs, Q_ROWS, HEAD_DIM), F32), jax.ShapeDtypeStruct((bs, Q_ROWS, nc), F32)),
        (_per_seq((Q_ROWS, HEAD_DIM)), _per_seq((Q_ROWS, nc))),
        [pltpu.VMEM((2, 256, ck), F32), pltpu.SemaphoreType.DMA((2,)), pltpu.VMEM((n_pages // ppc, 256, bpc), F32)])


def _nsa_sample_attn_kernel(pt_ref, q_ref, new_ref, blk_ref, exp_ref, oc_ref, gate_ref, win_ref, wnew_ref,
                            pool_ref, o_ref, buf, sem, m_scr, l_scr, acc_scr, *, ppc):
    c = pl.program_id(1)
    slot = _stream_chunk(pt_ref, [pool_ref], [buf], [sem], ppc)
    q = q_ref[0]

    @pl.when(c == 0)
    def _():
        logit, v = _new_row_terms(q, new_ref[0])
        m_scr[...] = logit
        l_scr[...] = jnp.ones((Q_ROWS, 1), F32)
        acc_scr[...] = v

    picked = jnp.dot(blk_ref[0, 0].astype(BF16), exp_ref[...], preferred_element_type=F32) > 0.5
    m, l, acc = _gqa_decode_chunk(q, buf[slot], picked, (m_scr[...], l_scr[...], acc_scr[...]))
    m_scr[...], l_scr[...], acc_scr[...] = m, l, acc

    @pl.when(c == pl.num_programs(1) - 1)
    def _():
        o_s = acc / l
        logit, v = _new_row_terms(q, wnew_ref[0])
        win = win_ref[0]
        every = jnp.full((1, win.shape[1]), True)
        _, lw, accw = _gqa_decode_chunk(q, win, every, (logit, jnp.ones((Q_ROWS, 1), F32), v))
        g = gate_ref[0]
        o_ref[0] = (g[:, 0:1] * oc_ref[0] + g[:, 1:2] * o_s + g[:, 2:3] * (accw / lw)).astype(BF16)


def nsa_sample_attn(page_table, q, new_row, blk, expand, o_c, gates, win_state, win_new, sel_pool, ppc):
    bs, n_pages = page_table.shape
    ck = ppc * PAGE_SIZE
    bpc = ck // SEL_BLOCK
    nbuf = win_state.shape[2]
    kern = functools.partial(_nsa_sample_attn_kernel, ppc=ppc)
    return _paged_call(
        kern, "nsa_sample_attn", page_table, n_pages // ppc,
        [q, new_row, blk, expand, o_c, gates, win_state, win_new],
        [_per_seq((Q_ROWS, HEAD_DIM)), _per_seq((1, 256)),
         pl.BlockSpec((1, 1, Q_ROWS, bpc), lambda b, c, pt: (b, c, 0, 0)),
         pl.BlockSpec((bpc, ck), lambda b, c, pt: (0, 0)),
         _per_seq((Q_ROWS, HEAD_DIM)), _per_seq((Q_ROWS, 3)), _per_seq((256, nbuf)), _per_seq((1, 256))],
        [sel_pool],
        jax.ShapeDtypeStruct((bs, Q_ROWS, HEAD_DIM), BF16), _per_seq((Q_ROWS, HEAD_DIM)),
        [pltpu.VMEM((2, 256, ck), F32), pltpu.SemaphoreType.DMA((2,)),
         pltpu.VMEM((Q_ROWS, 1), F32), pltpu.VMEM((Q_ROWS, 1), F32), pltpu.VMEM((Q_ROWS, HEAD_DIM), F32)])


def _mla_sample_attn_kernel(pt_ref, ql_ref, qp_ref, cn_ref, pn_ref, ckv_pool, kpe_pool, o_ref,
                            cbuf, pbuf, csem, psem, m_scr, l_scr, acc_scr, *, ppc):
    c = pl.program_id(1)
    slot = _stream_chunk(pt_ref, [ckv_pool, kpe_pool], [cbuf, pbuf], [csem, psem], ppc, (False, True))
    scale = (C_NOPE + C_ROPE) ** -0.5
    ql, qp = ql_ref[0], qp_ref[0]

    @pl.when(c == 0)
    def _():
        cn = cn_ref[0].astype(BF16).astype(F32)
        pn = pn_ref[0].astype(BF16).astype(F32)
        logit = (jnp.sum(ql.astype(F32) * cn, axis=1, keepdims=True)
                 + jnp.sum(qp.astype(F32) * pn, axis=1, keepdims=True)) * scale
        m_scr[...] = jnp.broadcast_to(logit, (C_HEADS, LANES))
        l_scr[...] = jnp.ones((C_HEADS, LANES), F32)
        acc_scr[...] = jnp.broadcast_to(cn, (C_HEADS, C_KV_RANK))

    cb = cbuf[slot].astype(BF16)
    pb = pbuf[slot].astype(BF16)
    logits = (lax.dot_general(ql, cb, _NT, preferred_element_type=F32)
              + jnp.dot(qp, pb, preferred_element_type=F32)) * scale
    m, l, acc = _online_softmax_step((m_scr[...], l_scr[...], acc_scr[...]), logits, cb)
    m_scr[...], l_scr[...], acc_scr[...] = m, l, acc

    @pl.when(c == pl.num_programs(1) - 1)
    def _():
        o_ref[0] = acc / _lanes_to(l, C_KV_RANK)


def mla_sample_attn(page_table, q_lat, q_pe, ckv_new, kpe_new, ckv_pool, kpe_pool, ppc):
    bs, n_pages = page_table.shape
    ck = ppc * PAGE_SIZE
    kern = functools.partial(_mla_sample_attn_kernel, ppc=ppc)
    return _paged_call(
        kern, "mla_sample_attn", page_table, n_pages // ppc, [q_lat, q_pe, ckv_new, kpe_new],
        [_per_seq((C_HEADS, C_KV_RANK)), _per_seq((C_HEADS, C_ROPE)), _per_seq((1, C_KV_RANK)),
         _per_seq((1, C_ROPE))],
        [ckv_pool, kpe_pool],
        jax.ShapeDtypeStruct((bs, C_HEADS, C_KV_RANK), F32), _per_seq((C_HEADS, C_KV_RANK)),
        [pltpu.VMEM((2, ck, C_KV_RANK), F32), pltpu.VMEM((2, C_ROPE, ck), F32),
         pltpu.SemaphoreType.DMA((2,)), pltpu.SemaphoreType.DMA((2,)),
         pltpu.VMEM((C_HEADS, LANES), F32), pltpu.VMEM((C_HEADS, LANES), F32),
         pltpu.VMEM((C_HEADS, C_KV_RANK), F32)])


def _head_bmm_kernel(x_ref, w_ref, o_ref):
    o_ref[0] = jnp.dot(x_ref[0].astype(BF16), w_ref[0].astype(BF16), preferred_element_type=F32).astype(o_ref.dtype)


def head_bmm(x, w):
    h, b, k = x.shape
    n = w.shape[2]
    return pl.pallas_call(
        _head_bmm_kernel,
        grid=(h,),
        in_specs=[pl.BlockSpec((1, b, k), lambda i: (i, 0, 0)), pl.BlockSpec((1, k, n), lambda i: (i, 0, 0))],
        out_specs=pl.BlockSpec((1, b, n), lambda i: (i, 0, 0)),
        out_shape=jax.ShapeDtypeStruct((h, b, n), BF16),
        compiler_params=_cparams(("parallel",)),
        name="head_bmm",
    )(x, w)


TM = 512
TQ_SPARSE = 256
TK_SPARSE = 512
TQ_MLA, TK_MLA = 512, 512


def kernel(x_prompt, x_sample, cache_dsa_kv, cache_dsa_idx, cache_nsa_cmp_kv, cache_nsa_sel_kv, state_nsa_win_kv,
           cache_mla_ckv, cache_mla_kpe, page_table, ln_g, ln_b, ffn_w_gate, ffn_w_up, ffn_w_down, ab_w_in,
           ab_w_out, mla_w_down, mla_g_q, mla_g_kv, mla_w_uq, mla_w_uk, mla_w_uv, mla_w_out):
    batch, seq, d = x_prompt.shape
    n = batch * seq
    tm = min(TM, n)
    tq_s, tk_s = min(TQ_SPARSE, seq), min(TK_SPARSE, seq)
    tq_m, tk_m = min(TQ_MLA, seq), min(TK_MLA, seq)
    pos = jnp.arange(seq)

    bs, dec_seq, _ = x_sample.shape
    n_pool = cache_dsa_kv.shape[0]
    n_pages = page_table.shape[1]
    past = n_pages * PAGE_SIZE
    nbuf = state_nsa_win_kv.shape[1]
    ppc = min(PAGES_PER_CHUNK, n_pages)
    n_chunks = n_pages // ppc
    ck = ppc * PAGE_SIZE
    assert dec_seq == 1 and n_pages % ppc == 0 and past % SEL_BLOCK == 0 and nbuf <= min(WINDOW, past)
    assert (past // CMP_BLOCK) % LANES == 0
    spos = jnp.full((bs,), past, I32)

    yp = x_prompt.reshape(n, d)
    ys = x_sample.reshape(bs, d)
    ffn_w = [[_prep_ffn(ffn_w_gate[l, j], ffn_w_up[l, j], ffn_w_down[l, j]) for j in range(2)] for l in range(DEPTH)]

    g, b = ln_g[0], ln_b[0]
    yp = ffn_ln(yp, *ffn_w[0][0], g[0], b[0], tm)
    ys = ffn_ln(ys, *ffn_w[0][0], g[0], b[0], bs)

    w_in_b = _prep_ab_w_in(ab_w_in)
    w_out_b = ab_w_out.astype(BF16)
    half = A_HEADS * HEAD_DIM
    tab_c, tab_s = _rope_tables(pos, HEAD_DIM, 0, LANES)
    q_all, akv, ckv, skv, wkv, ik, misc = ab_proj(yp, w_in_b, tab_c, tab_s, tm)
    o_a = dsa_prompt(q_all, misc, ik, akv, batch, seq, tq_s, tk_s)
    o_b = nsa_prompt(q_all, misc, ckv, skv, wkv, batch, seq, tq_s, tk_s)
    yp = proj_ln(yp, [o_a, o_b], [w_out_b[:half], w_out_b[half:]], g[1], b[1], tm)

    stab_c, stab_s = _rope_tables(spos, HEAD_DIM, 0, LANES)
    q_all_s, akv_s, ckv_s, skv_s, wkv_s, ik_s, misc_s = ab_proj(ys, w_in_b, stab_c, stab_s, bs)
    heads3 = lambda a: a.reshape(bs, Q_ROWS, HEAD_DIM)
    aq_s, iq_s, bq_s = heads3(q_all_s[:, 0:512]), heads3(q_all_s[:, 512:1024]), heads3(q_all_s[:, 1024:1536])
    iw_s = misc_s[:, MISC_IW:MISC_IW + IDX_HEADS].reshape(bs, IDX_HEADS, 1)
    gates_s = misc_s[:, MISC_GATE:MISC_GATE + 3 * B_HEADS].reshape(bs, 3, B_HEADS).transpose(0, 2, 1)
    row3 = lambda a: a.reshape(bs, 1, a.shape[-1])
    rows_last = lambda a: jnp.moveaxis(a, 1, -1).reshape(a.shape[0], -1, a.shape[1])
    bpc = ck // SEL_BLOCK
    expand = (jnp.arange(ck)[None, :] // SEL_BLOCK == jnp.arange(bpc)[:, None]).astype(BF16)

    keys, meta = dsa_sample_index(page_table, iq_s, iw_s, row3(ik_s), rows_last(cache_dsa_idx), ppc)
    o_a_s = dsa_sample_attn(page_table, aq_s, row3(akv_s), keys, meta, rows_last(cache_dsa_kv), ppc)
    o_c_s, blk_sel = nsa_sample_cmp(page_table, bq_s, expand, rows_last(cache_nsa_cmp_kv), ppc)
    blk_heads = jnp.repeat(blk_sel[:, 0:B_KV_HEADS], B_HEADS // B_KV_HEADS, axis=1)
    blk_heads = blk_heads.reshape(bs, Q_ROWS, n_chunks, bpc).transpose(0, 2, 1, 3)
    o_b_s = nsa_sample_attn(page_table, bq_s, row3(skv_s), blk_heads, expand, o_c_s, gates_s,
                            rows_last(state_nsa_win_kv), row3(wkv_s), rows_last(cache_nsa_sel_kv), ppc)
    ys = proj_ln(ys, [o_a_s.reshape(bs, half), o_b_s.reshape(bs, half)], [w_out_b[:half], w_out_b[half:]],
                 g[1], b[1], bs)

    yp = ffn_ln(yp, *ffn_w[0][1], g[2], b[2], tm)
    ys = ffn_ln(ys, *ffn_w[0][1], g[2], b[2], bs)

    kv5 = lambda a, t: a.reshape(-1, t, 2, 2, HEAD_DIM)
    dsa_kv_p, cmp_kv_p, sel_kv_p = kv5(akv, seq), kv5(ckv, seq), kv5(skv, seq)
    win_kv_p = kv5(wkv, seq)[:, -min(WINDOW, seq):]
    dsa_idx_p = ik.reshape(batch, seq, IDX_DIM)
    dsa_kv_s, cmp_kv_s, sel_kv_s = kv5(akv_s, 1), kv5(ckv_s, 1), kv5(skv_s, 1)
    win_kv_s = jnp.concatenate([state_nsa_win_kv, kv5(wkv_s, 1)], axis=1)[:, -nbuf:]
    dsa_idx_s = ik_s.reshape(bs, 1, IDX_DIM)

    g, b = ln_g[1], ln_b[1]
    yp = ffn_ln(yp, *ffn_w[1][0], g[0], b[0], tm)
    ys = ffn_ln(ys, *ffn_w[1][0], g[0], b[0], bs)

    w_mla_out_b = mla_w_out.astype(BF16)
    mtab_c, mtab_s = _rope_tables(pos, C_ROPE, ROPE_LO, ROPE_LO + C_ROPE)
    w_down_r, uq, uk, uv = _prep_mla(mla_w_down, mla_w_uq, mla_w_uk, mla_w_uv)
    qcat, kcat, v, ckv_p, kpe_p = mla_proj(yp, w_down_r, uq, uk, uv, mla_g_q, mla_g_kv, mtab_c, mtab_s, tm)
    o_m = mla_attn(qcat, kcat, v, batch, seq, tq_m, tk_m)
    yp = proj_ln(yp, [o_m], [w_mla_out_b], g[1], b[1], tm)

    smtab_c, smtab_s = _rope_tables(spos, C_ROPE, ROPE_LO, ROPE_LO + C_ROPE)
    qcat_s, _, _, ckv_s1, kpe_s1 = mla_proj(ys, w_down_r, uq, uk, uv, mla_g_q, mla_g_kv, smtab_c, smtab_s, bs)
    qh = qcat_s.reshape(bs, C_HEADS, LANES)
    q_nope, q_pe = qh[:, :, 0:C_NOPE], qh[:, :, ROPE_LO:ROPE_LO + C_ROPE]
    q_lat = head_bmm(q_nope.transpose(1, 0, 2), mla_w_uk.transpose(1, 2, 0)).transpose(1, 0, 2)
    o_lat = mla_sample_attn(page_table, q_lat, q_pe, row3(ckv_s1), row3(kpe_s1), cache_mla_ckv,
                            rows_last(cache_mla_kpe), ppc)
    o_s = head_bmm(o_lat.transpose(1, 0, 2), mla_w_uv.transpose(1, 0, 2)).transpose(1, 0, 2)
    ys = proj_ln(ys, [o_s.reshape(bs, C_HEADS * C_VDIM)], [w_mla_out_b], g[1], b[1], bs)

    yp = ffn_ln(yp, *ffn_w[1][1], g[2], b[2], tm)
    ys = ffn_ln(ys, *ffn_w[1][1], g[2], b[2], bs)

    return (yp.reshape(batch, seq, d), ys.reshape(bs, 1, d), dsa_kv_p, dsa_kv_s, dsa_idx_p, dsa_idx_s,
            cmp_kv_p, cmp_kv_s, sel_kv_p, sel_kv_s, win_kv_p, win_kv_s,
            ckv_p.reshape(batch, seq, C_KV_RANK), ckv_s1.reshape(bs, 1, C_KV_RANK),
            kpe_p.reshape(batch, seq, C_ROPE), kpe_s1.reshape(bs, 1, C_ROPE))
```
